```python
import jax, jax.numpy as jnp
from jax import lax
import numpy as np

D_MODEL = 1024
BATCH = 1
SEQ = 16384
DEPTH = 4

D_MIX = D_MODEL
GLA_HEADS = 4
GLA_DV = 128
GLA_DK = 64
GLA_GATE_RANK = 16
GLA_GATE_NORMALIZER = 16.0
GLA_CHUNK = 64
GLA_WIDTH = GLA_HEADS * GLA_DV
LRU_WIDTH = D_MIX - GLA_WIDTH
LRU_BLOCKS = 8
LRU_BLOCK = LRU_WIDTH // LRU_BLOCKS
LRU_CONV = 4
LRU_C = 8.0
FFN_HIDDEN = 3 * D_MODEL
FFN_CONV = 3
EPS = 1e-6
Q_COLS = GLA_HEADS * GLA_DK
K_COLS = GLA_HEADS * GLA_DK
V_COLS = GLA_WIDTH
G_COLS = GLA_WIDTH
A_COLS = GLA_GATE_RANK
X_COLS = LRU_WIDTH
Y_COLS = LRU_WIDTH
D_IN = Q_COLS + K_COLS + V_COLS + G_COLS + A_COLS + X_COLS + Y_COLS
SPLITS = [Q_COLS, Q_COLS + K_COLS, Q_COLS + K_COLS + V_COLS,
          Q_COLS + K_COLS + V_COLS + G_COLS,
          Q_COLS + K_COLS + V_COLS + G_COLS + A_COLS,
          Q_COLS + K_COLS + V_COLS + G_COLS + A_COLS + X_COLS]

kernel_name = "hymba_gla_rglru_convffn_trunk"


def rmsnorm(x, g):
    xf = x.astype(jnp.float32)
    y = xf * lax.rsqrt(jnp.mean(xf * xf, axis=-1, keepdims=True) + EPS)
    return (y * g.astype(jnp.float32)).astype(x.dtype)


def causal_dwconv(x, w, b):
    K = w.shape[0]
    T = x.shape[1]
    xp = jnp.pad(x, ((0, 0), (K - 1, 0), (0, 0)))
    out = b
    for j in range(K):
        out = out + xp[:, j:j + T, :] * w[j]
    return out


def gla_heads(q, k, v, g_out, gate_lr, w2, b2, norm_g):
    f32 = jnp.float32
    Bsz, T, _ = q.shape
    N = T // GLA_CHUNK
    C = GLA_CHUNK
    log_alpha = jax.nn.log_sigmoid((gate_lr @ w2 + b2).astype(f32)) / GLA_GATE_NORMALIZER

    def chunks(t, d):
        return t.astype(f32).reshape(Bsz, N, C, GLA_HEADS, d).transpose(0, 3, 1, 2, 4)

    qc = chunks(q, GLA_DK) * (GLA_DK ** -0.5)
    kc = chunks(k, GLA_DK)
    vc = chunks(v, GLA_DV)
    gc = chunks(log_alpha, GLA_DK)
    b = jnp.cumsum(gc, axis=3)
    b_last = b[:, :, :, -1:, :]
    q_i = qc * jnp.exp(b)
    k_i = kc * jnp.exp(-b)
    k_dec = kc * jnp.exp(b_last - b)
    U = jnp.einsum('bhncd,bhnce->bhnde', k_dec, vc)
    decay = jnp.exp(b_last[:, :, :, 0, :])

    def step(S, inp):
        d_n, u_n = inp
        return d_n[..., None] * S + u_n, S

    S0 = jnp.zeros((Bsz, GLA_HEADS, GLA_DK, GLA_DV), f32)
    _, S_prev = lax.scan(step, S0, (jnp.moveaxis(decay, 2, 0), jnp.moveaxis(U, 2, 0)))
    S_prev = jnp.moveaxis(S_prev, 0, 2)
    mask = jnp.tril(jnp.ones((C, C), dtype=bool))
    A = jnp.where(mask, jnp.einsum('bhncd,bhnsd->bhncs', q_i, k_i), 0.0)
    o = (jnp.einsum('bhncs,bhnse->bhnce', A, vc)
         + jnp.einsum('bhncd,bhnde->bhnce', q_i, S_prev))
    o = o.transpose(0, 2, 3, 1, 4).reshape(Bsz, T, GLA_HEADS, GLA_DV)
    o = rmsnorm(o, norm_g)
    o = o * jax.nn.silu(g_out.astype(f32).reshape(Bsz, T, GLA_HEADS, GLA_DV))
    return o.reshape(Bsz, T, GLA_WIDTH).astype(q.dtype)


def rglru_heads(xr, xg, conv_w, conv_b, wa, ba, wx, bx, lam):
    f32 = jnp.float32
    Bsz, T, W = xr.shape
    xc = causal_dwconv(xr, conv_w, conv_b).astype(f32)
    xb = xc.reshape(Bsz, T, LRU_BLOCKS, LRU_BLOCK)
    r_gate = jax.nn.sigmoid(jnp.einsum('btnd,nde->btne', xb, wa.astype(f32)).reshape(Bsz, T, W) + ba)
    i_gate = jax.nn.sigmoid(jnp.einsum('btnd,nde->btne', xb, wx.astype(f32)).reshape(Bsz, T, W) + bx)
    log_a = -LRU_C * r_gate * jax.nn.softplus(-lam.astype(f32))
    a = jnp.exp(log_a)
    mult = jnp.sqrt(-jnp.expm1(2.0 * log_a))
    mult = jnp.where(jnp.arange(T)[None, :, None] == 0, 1.0, mult)
    u = mult * (i_gate * xc)

    def combine(left, right):
        a1, b1 = left
        a2, b2 = right
        return a1 * a2, a2 * b1 + b2

    _, h = lax.associative_scan(combine, (a, u), axis=1)
    y = h * jax.nn.gelu(xg.astype(f32))
    return y.astype(xr.dtype)


def conv_ffn(u, w_in, conv_w, conv_b, w_down):
    z = causal_dwconv(u @ w_in, conv_w, conv_b)
    a, gt = jnp.split(z, 2, axis=-1)
    return (jax.nn.gelu(a) * gt) @ w_down


def setup_inputs(seed: int = 0) -> dict:
    key = jax.random.key(seed)
    ks = jax.random.split(key, 24)
    f32 = jnp.float32
    nrm = lambda k, shape, s: jax.random.normal(k, shape, f32) * s
    u_lam = jax.random.uniform(ks[14], (DEPTH, LRU_WIDTH), f32, 0.9, 0.999)
    a0 = u_lam ** (1.0 / LRU_C)
    return {
        "x": jax.random.normal(ks[0], (BATCH, SEQ, D_MODEL), f32),
        "ln_mix": 1.0 + nrm(ks[1], (DEPTH, D_MODEL), 0.02),
        "w_in": nrm(ks[2], (DEPTH, D_MODEL, D_IN), D_MODEL ** -0.5),
        "gla_gate_w2": nrm(ks[3], (DEPTH, GLA_GATE_RANK, Q_COLS), GLA_GATE_RANK ** -0.5),
        "gla_gate_b": nrm(ks[4], (DEPTH, Q_COLS), 0.1),
        "gla_norm": 1.0 + nrm(ks[5], (DEPTH, GLA_DV), 0.02),
        "lru_conv_w": nrm(ks[6], (DEPTH, LRU_CONV, LRU_WIDTH), LRU_CONV ** -0.5),
        "lru_conv_b": nrm(ks[7], (DEPTH, LRU_WIDTH), 0.02),
        "lru_wa": nrm(ks[8], (DEPTH, LRU_BLOCKS, LRU_BLOCK, LRU_BLOCK), LRU_BLOCK ** -0.5),
        "lru_ba": nrm(ks[9], (DEPTH, LRU_WIDTH), 0.02),
        "lru_wx": nrm(ks[10], (DEPTH, LRU_BLOCKS, LRU_BLOCK, LRU_BLOCK), LRU_BLOCK ** -0.5),
        "lru_bx": nrm(ks[11], (DEPTH, LRU_WIDTH), 0.02),
        "lru_lambda": jnp.log(a0) - jnp.log1p(-a0),
        "w_out": nrm(ks[12], (DEPTH, D_MIX, D_MODEL), D_MIX ** -0.5),
        "ln_ffn": 1.0 + nrm(ks[13], (DEPTH, D_MODEL), 0.02),
        "ffn_w_in": nrm(ks[15], (DEPTH, D_MODEL, 2 * FFN_HIDDEN), D_MODEL ** -0.5),
        "ffn_conv_w": nrm(ks[16], (DEPTH, FFN_CONV, 2 * FFN_HIDDEN), FFN_CONV ** -0.5),
        "ffn_conv_b": nrm(ks[17], (DEPTH, 2 * FFN_HIDDEN), 0.02),
        "ffn_w_down": nrm(ks[18], (DEPTH, FFN_HIDDEN, D_MODEL), FFN_HIDDEN ** -0.5),
        "ln_final": 1.0 + nrm(ks[19], (D_MODEL,), 0.02),
    }


def reference(x, ln_mix, w_in, gla_gate_w2, gla_gate_b, gla_norm, lru_conv_w, lru_conv_b,
              lru_wa, lru_ba, lru_wx, lru_bx, lru_lambda, w_out, ln_ffn, ffn_w_in,
              ffn_conv_w, ffn_conv_b, ffn_w_down, ln_final):
    h = x
    for l in range(DEPTH):
        u = rmsnorm(h, ln_mix[l])
        proj = u @ w_in[l]
        q, k, v, g_out, gate_lr, lru_x, lru_g = jnp.split(proj, SPLITS, axis=-1)
        y_gla = gla_heads(q, k, v, g_out, gate_lr, gla_gate_w2[l], gla_gate_b[l], gla_norm[l])
        y_lru = rglru_heads(lru_x, lru_g, lru_conv_w[l], lru_conv_b[l], lru_wa[l], lru_ba[l],
                            lru_wx[l], lru_bx[l], lru_lambda[l])
        h = h + jnp.concatenate([y_gla, y_lru], axis=-1) @ w_out[l]
        u = rmsnorm(h, ln_ffn[l])
        h = h + conv_ffn(u, ffn_w_in[l], ffn_conv_w[l], ffn_conv_b[l], ffn_w_down[l])
    return rmsnorm(h, ln_final)
```

```python
import functools
import math

import jax
import jax.numpy as jnp
from jax import lax
from jax.experimental import pallas as pl
from jax.experimental.pallas import tpu as pltpu

D_MODEL = 1024
GLA_HEADS = 4
GLA_DV = 128
GLA_DK = 64
GLA_GATE_RANK = 16
GLA_GATE_NORMALIZER = 16.0
GLA_CHUNK = 64
GLA_WIDTH = GLA_HEADS * GLA_DV
QK_COLS = GLA_HEADS * GLA_DK
LRU_WIDTH = D_MODEL - GLA_WIDTH
LRU_BLOCKS = 8
LRU_BLOCK = LRU_WIDTH // LRU_BLOCKS
LRU_CONV = 4
LRU_C = 8.0
FFN_HIDDEN = 3 * D_MODEL
FFN_CONV = 3
EPS = 1e-6

LANES = 128
SUBLANES = 8
MXU_DIM = 256
VMEM_LIMIT_BYTES = 56 * 1024 * 1024

Q0, K0, V0, G0, X0, Y0 = 0, 256, 512, 1024, 1536, 2048
MAIN_COLS = 2560
GATE_PAD = LANES
FFN_COL_BLOCK = 512
TIME_TILE = 512

F32 = jnp.float32
BF16 = jnp.bfloat16


def _dot(a, b):
    return jnp.dot(a, b, preferred_element_type=F32)


def _dot_nt(a, b):
    return lax.dot_general(a, b, (((1,), (1,)), ((), ())), preferred_element_type=F32)


def _dot_tn(a, b):
    return lax.dot_general(a, b, (((0,), (0,)), ((), ())), preferred_element_type=F32)


def _rmsnorm(x, g):
    return x * lax.rsqrt(jnp.mean(x * x, axis=-1, keepdims=True) + EPS) * g


def _softplus(x):
    return jnp.maximum(x, 0.0) + jnp.log1p(jnp.exp(-jnp.abs(x)))


def _sigmoid(x):
    return 1.0 / (1.0 + jnp.exp(-x))


def _gelu_tanh(x):
    c = math.sqrt(2.0 / math.pi)
    return x * (0.5 * (1.0 + jnp.tanh(c * (x + 0.044715 * (x * x * x)))))


def _split3_bf16(x):
    hi = x.astype(BF16)
    r1 = x - hi.astype(F32)
    mid = r1.astype(BF16)
    lo = (r1 - mid.astype(F32)).astype(BF16)
    return hi, mid, lo


def _mixer_kernel(h_ref, ln_ref, wmain_ref, wgate_ref, w2_ref, b2_ref, gnorm_ref,
                  cw_ref, cb_ref, wa_ref, ba_ref, wx_ref, bx_ref, lam_ref, wout_ref,
                  out_ref,
                  proj_ref, la_ref, y_ref, st_ref, xbuf_ref, hlast_ref, *, tt):
    i = pl.program_id(0)

    @pl.when(i == 0)
    def _init():
        st_ref[...] = jnp.zeros_like(st_ref)
        xbuf_ref[0:SUBLANES, :] = jnp.zeros((SUBLANES, LRU_WIDTH), F32)
        hlast_ref[...] = jnp.zeros_like(hlast_ref)

    h_in = h_ref[...]
    ub = _rmsnorm(h_in, ln_ref[...]).astype(BF16)
    proj_ref[...] = _dot(ub, wmain_ref[...])
    glr = _dot(ub, wgate_ref[...])
    gate = _dot(glr.astype(BF16), w2_ref[...]) + b2_ref[...]
    la_ref[...] = -_softplus(-gate) * (1.0 / GLA_GATE_NORMALIZER)

    c_row = lax.broadcasted_iota(jnp.int32, (GLA_CHUNK, GLA_CHUNK), 0)
    c_col = lax.broadcasted_iota(jnp.int32, (GLA_CHUNK, GLA_CHUNK), 1)
    causal = c_row >= c_col
    tril_bf = jnp.where(causal, 1.0, 0.0).astype(BF16)
    gnorm = gnorm_ref[...]

    def chunk_body(c, carry):
        r0 = pl.multiple_of(c * GLA_CHUNK, GLA_CHUNK)
        rows = pl.ds(r0, GLA_CHUNK)
        la = la_ref[rows, :]
        la_hi, la_mid, la_lo = _split3_bf16(la)
        b = _dot(tril_bf, la_hi) + _dot(tril_bf, la_mid) + _dot(tril_bf, la_lo)
        b_last = b[GLA_CHUNK - 1:GLA_CHUNK, :]
        q = proj_ref[rows, Q0:Q0 + QK_COLS]
        k = proj_ref[rows, K0:K0 + QK_COLS]
        v = proj_ref[rows, V0:V0 + GLA_WIDTH]
        g = proj_ref[rows, G0:G0 + GLA_WIDTH]
        q_i = ((q * (GLA_DK ** -0.5)) * jnp.exp(b)).astype(BF16)
        k_i = (k * jnp.exp(-b)).astype(BF16)
        k_dec = (k * jnp.exp(b_last - b)).astype(BF16)
        decay = jnp.exp(b_last)
        vb = v.astype(BF16)
        st = st_ref[...]
        st_b = st.astype(BF16)
        ys = []
        st_new = []
        for hd in range(GLA_HEADS):
            ks = slice(hd * GLA_DK, (hd + 1) * GLA_DK)
            vs = slice(hd * GLA_DV, (hd + 1) * GLA_DV)
            a = jnp.where(causal, _dot_nt(q_i[:, ks], k_i[:, ks]), 0.0).astype(BF16)
            o = _dot(a, vb[:, vs]) + _dot_nt(q_i[:, ks], st_b[:, ks])
            st_new.append(decay[:, ks] * st[:, ks] + _dot_tn(vb[:, vs], k_dec[:, ks]))
            o = _rmsnorm(o, gnorm)
            gh = g[:, vs]
            ys.append(o * (gh * _sigmoid(gh)))
        st_ref[...] = jnp.concatenate(st_new, axis=1)
        y_ref[rows, 0:GLA_WIDTH] = jnp.concatenate(ys, axis=1).astype(BF16)
        return carry

    lax.fori_loop(0, tt // GLA_CHUNK, chunk_body, 0)

    xbuf_ref[SUBLANES:SUBLANES + tt, :] = proj_ref[:, X0:X0 + LRU_WIDTH]
    cw = cw_ref[...]
    xc = cb_ref[...] + jnp.zeros((tt, LRU_WIDTH), F32)
    for j in range(LRU_CONV):
        off = SUBLANES - (LRU_CONV - 1) + j
        xc = xc + xbuf_ref[off:off + tt, :] * cw[j:j + 1, :]
    xbuf_ref[0:SUBLANES, :] = xbuf_ref[tt:tt + SUBLANES, :]
    xcb = xc.astype(BF16)
    half = LRU_WIDTH // 2
    ra = jnp.concatenate([_dot(xcb[:, :half], wa_ref[0]), _dot(xcb[:, half:], wa_ref[1])], axis=1)
    rx = jnp.concatenate([_dot(xcb[:, :half], wx_ref[0]), _dot(xcb[:, half:], wx_ref[1])], axis=1)
    r_gate = _sigmoid(ra + ba_ref[...])
    i_gate = _sigmoid(rx + bx_ref[...])
    log_a = (-LRU_C) * r_gate * _softplus(-lam_ref[...])
    a = jnp.exp(log_a)
    th = jnp.tanh(log_a)
    mult = jnp.sqrt((-2.0 * th) / (1.0 - th))
    row = lax.broadcasted_iota(jnp.int32, (tt, LRU_WIDTH), 0)
    mult = jnp.where((row + i * tt) == 0, 1.0, mult)
    u = mult * (i_gate * xc)
    u = jnp.where(row == 0, u + a * hlast_ref[...], u)
    d = 1
    while d < tt:
        keep = row >= d
        u_s = jnp.where(keep, pltpu.roll(u, d, axis=0), 0.0)
        u = a * u_s + u
        if 2 * d < tt:
            a = a * jnp.where(keep, pltpu.roll(a, d, axis=0), 1.0)
        d *= 2
    hlast_ref[...] = u[tt - 1:tt, :]
    xg = proj_ref[:, Y0:Y0 + LRU_WIDTH]
    y_ref[:, GLA_WIDTH:D_MODEL] = (u * _gelu_tanh(xg)).astype(BF16)

    out_ref[...] = h_in + _dot(y_ref[...], wout_ref[...])


def _ffn_kernel(h_ref, ln_ref, win_ref, cw_ref, cb_ref, wdown_ref, lnf_ref, out_ref,
                zbuf_ref, act_ref, *, tt, final_norm):
    i = pl.program_id(0)

    @pl.when(i == 0)
    def _init():
        zbuf_ref[0:SUBLANES, :] = jnp.zeros((SUBLANES, 2 * FFN_HIDDEN), F32)

    h_in = h_ref[...]
    ub = _rmsnorm(h_in, ln_ref[...]).astype(BF16)

    def conv(c0):
        cols = slice(c0, c0 + FFN_COL_BLOCK)
        zbuf_ref[SUBLANES:SUBLANES + tt, cols] = _dot(ub, win_ref[:, cols])
        z = cb_ref[:, cols] + jnp.zeros((tt, FFN_COL_BLOCK), F32)
        for j in range(FFN_CONV):
            off = SUBLANES - (FFN_CONV - 1) + j
            z = z + zbuf_ref[off:off + tt, cols] * cw_ref[j:j + 1, cols]
        zbuf_ref[0:SUBLANES, cols] = zbuf_ref[tt:tt + SUBLANES, cols]
        return z

    for blk in range(FFN_HIDDEN // FFN_COL_BLOCK):
        c0 = blk * FFN_COL_BLOCK
        za = conv(c0)
        zg = conv(FFN_HIDDEN + c0)
        act_ref[:, c0:c0 + FFN_COL_BLOCK] = (_gelu_tanh(za) * zg).astype(BF16)

    out = h_in + _dot(act_ref[...], wdown_ref[...])
    if final_norm:
        out = _rmsnorm(out, lnf_ref[...])
    out_ref[...] = out


def _const_spec(shape):
    zeros = (0,) * len(shape)
    return pl.BlockSpec(shape, lambda i: zeros, pipeline_mode=pl.Buffered(1))


def _time_tile(t):
    tt = min(TIME_TILE, t)
    assert t % tt == 0 and tt % GLA_CHUNK == 0
    return tt


def _mixer_call(h, ln, wmain, wgate, w2, b2, gnorm, cw, cb, wa, ba, wx, bx, lam, wout):
    t = h.shape[0]
    tt = _time_tile(t)
    consts = (ln, wmain, wgate, w2, b2, gnorm, cw, cb, wa, ba, wx, bx, lam, wout)
    row_spec = pl.BlockSpec((tt, D_MODEL), lambda i: (i, 0))
    return pl.pallas_call(
        functools.partial(_mixer_kernel, tt=tt),
        grid=(t // tt,),
        in_specs=[row_spec] + [_const_spec(c.shape) for c in consts],
        out_specs=row_spec,
        out_shape=jax.ShapeDtypeStruct((t, D_MODEL), F32),
        scratch_shapes=[
            pltpu.VMEM((tt, MAIN_COLS), F32),
            pltpu.VMEM((tt, QK_COLS), F32),
            pltpu.VMEM((tt, D_MODEL), BF16),
            pltpu.VMEM((GLA_DV, QK_COLS), F32),
            pltpu.VMEM((SUBLANES + tt, LRU_WIDTH), F32),
            pltpu.VMEM((1, LRU_WIDTH), F32),
        ],
        compiler_params=pltpu.CompilerParams(
            dimension_semantics=("arbitrary",), vmem_limit_bytes=VMEM_LIMIT_BYTES),
        name="mixer",
    )(h, *consts)


def _ffn_call(h, ln, win, cw, cb, wdown, lnf, final_norm):
    t = h.shape[0]
    tt = _time_tile(t)
    consts = (ln, win, cw, cb, wdown, lnf)
    row_spec = pl.BlockSpec((tt, D_MODEL), lambda i: (i, 0))
    return pl.pallas_call(
        functools.partial(_ffn_kernel, tt=tt, final_norm=final_norm),
        grid=(t // tt,),
        in_specs=[row_spec] + [_const_spec(c.shape) for c in consts],
        out_specs=row_spec,
        out_shape=jax.ShapeDtypeStruct((t, D_MODEL), F32),
        scratch_shapes=[
            pltpu.VMEM((SUBLANES + tt, 2 * FFN_HIDDEN), F32),
            pltpu.VMEM((tt, FFN_HIDDEN), BF16),
        ],
        compiler_params=pltpu.CompilerParams(
            dimension_semantics=("arbitrary",), vmem_limit_bytes=VMEM_LIMIT_BYTES),
        name="ffn",
    )(h, *consts)


def _block_diag_halves(w):
    per = MXU_DIM // LRU_BLOCK
    tiles = []
    for s in range(LRU_BLOCKS // per):
        tiles.append(jax.scipy.linalg.block_diag(*[w[s * per + j] for j in range(per)]))
    return jnp.stack(tiles).astype(BF16)


def kernel(x, ln_mix, w_in, gla_gate_w2, gla_gate_b, gla_norm, lru_conv_w, lru_conv_b,
           lru_wa, lru_ba, lru_wx, lru_bx, lru_lambda, w_out, ln_ffn, ffn_w_in,
           ffn_conv_w, ffn_conv_b, ffn_w_down, ln_final):
    bsz, t, d = x.shape
    assert bsz == 1 and d == D_MODEL
    depth = w_in.shape[0]
    h = x.reshape(t, d)
    a0 = Q0 + 2 * QK_COLS + 2 * GLA_WIDTH
    a1 = a0 + GLA_GATE_RANK
    row = lambda p: p.reshape(1, -1)
    for l in range(depth):
        wmain = jnp.concatenate([w_in[l][:, :a0], w_in[l][:, a1:]], axis=1).astype(BF16)
        wgate = jnp.pad(w_in[l][:, a0:a1], ((0, 0), (0, GATE_PAD - GLA_GATE_RANK))).astype(BF16)
        w2 = jnp.pad(gla_gate_w2[l], ((0, GATE_PAD - GLA_GATE_RANK), (0, 0))).astype(BF16)
        h = _mixer_call(
            h, row(ln_mix[l]), wmain, wgate, w2, row(gla_gate_b[l]), row(gla_norm[l]),
            lru_conv_w[l], row(lru_conv_b[l]), _block_diag_halves(lru_wa[l]), row(lru_ba[l]),
            _block_diag_halves(lru_wx[l]), row(lru_bx[l]), row(lru_lambda[l]),
            w_out[l].astype(BF16))
        h = _ffn_call(
            h, row(ln_ffn[l]), ffn_w_in[l].astype(BF16), ffn_conv_w[l], row(ffn_conv_b[l]),
            ffn_w_down[l].astype(BF16), row(ln_final), final_norm=(l == depth - 1))
    return h.reshape(bsz, t, d)
```

```python
import functools
import math

import jax
import jax.numpy as jnp
from jax import lax
from jax.experimental import pallas as pl
from jax.experimental.pallas import tpu as pltpu

D_MODEL = 1024
GLA_HEADS = 4
GLA_DV = 128
GLA_DK = 64
GLA_GATE_RANK = 16
GLA_GATE_NORMALIZER = 16.0
GLA_CHUNK = 64
GLA_WIDTH = GLA_HEADS * GLA_DV
QK_COLS = GLA_HEADS * GLA_DK
LRU_WIDTH = D_MODEL - GLA_WIDTH
LRU_BLOCKS = 8
LRU_BLOCK = LRU_WIDTH // LRU_BLOCKS
LRU_CONV = 4
LRU_C = 8.0
FFN_HIDDEN = 3 * D_MODEL
FFN_CONV = 3
EPS = 1e-6

LANES = 128
SUBLANES = 8
MXU_DIM = 256
VMEM_LIMIT_BYTES = 56 * 1024 * 1024

Q0, K0, V0, G0, X0, Y0 = 0, 256, 512, 1024, 1536, 2048
MAIN_COLS = 2560
QKVG_PIECES = ((0, 512), (512, 1024), (1024, 1280), (1280, 1536))
GATE_PAD = LANES
FFN_COL_BLOCK = 512
TIME_TILE = SUBLANES * GLA_CHUNK

F32 = jnp.float32
BF16 = jnp.bfloat16


def _dot(a, b):
    return jnp.dot(a, b, preferred_element_type=F32)


def _dot_nt(a, b):
    return lax.dot_general(a, b, (((1,), (1,)), ((), ())), preferred_element_type=F32)


def _dot_tn(a, b):
    return lax.dot_general(a, b, (((0,), (0,)), ((), ())), preferred_element_type=F32)


def _rmsnorm(x, g):
    return x * lax.rsqrt(jnp.mean(x * x, axis=-1, keepdims=True) + EPS) * g


def _softplus(x):
    return jnp.maximum(x, 0.0) + jnp.log1p(jnp.exp(-jnp.abs(x)))


def _sigmoid(x):
    return 1.0 / (1.0 + jnp.exp(-x))


def _gelu_tanh(x):
    c = math.sqrt(2.0 / math.pi)
    hx = 0.5 * x
    return hx + hx * jnp.tanh(x * (c + (c * 0.044715) * (x * x)))


def _interleave_rows(src, slab_ref):
    n, tt, _ = slab_ref.shape
    seg = tt // SUBLANES
    for c in range(n):
        for s in range(SUBLANES):
            slab_ref[c, pl.ds(s, seg, stride=SUBLANES), :] = (
                src[s * seg:(s + 1) * seg, c * LANES:(c + 1) * LANES])


def _time_ordered_block(slab_ref, c, s):
    seg = slab_ref.shape[1] // SUBLANES
    return slab_ref[c, pl.ds(s, seg, stride=SUBLANES), :]


def _slabs_value(slab_ref, lo=0, hi=None):
    hi = slab_ref.shape[0] if hi is None else hi
    return jnp.concatenate([slab_ref[c] for c in range(lo, hi)], axis=1)


def _segment_history(tail, prev_tail):
    first = lax.broadcasted_iota(jnp.int32, tail.shape, 1) == 0
    return jnp.where(first, pltpu.roll(prev_tail, 1, axis=1), pltpu.roll(tail, 1, axis=1))


def _causal_conv_interleaved(x3, before, w_ref, b_ref, cols):
    taps = w_ref.shape[0]
    hist = taps - 1
    seg = x3.shape[0]
    out = b_ref[:, cols] + x3 * w_ref[hist:taps, cols]
    for k in range(1, taps):
        xk = jnp.concatenate([before[hist - k:], x3[:seg - k]], axis=0)
        out = out + xk * w_ref[hist - k:hist - k + 1, cols]
    return out


def _mixer_kernel(h_ref, ln_ref, wmain_ref, wgate_ref, w2_ref, b2_ref, gnorm_ref,
                  cw_ref, cb_ref, wa_ref, ba_ref, wx_ref, bx_ref, lam_ref, wout_ref,
                  out_ref,
                  ub_ref, qkvg_ref, yg_ref, xil_ref, lail_ref, bil_ref, bnat_ref, qi_ref, ki_ref, kd_ref,
                  y_ref, st_ref, xtail_ref, hs_ref, ps_ref, hil_ref, hlast_ref, *, tt):
    i = pl.program_id(0)
    seg = tt // SUBLANES

    @pl.when(i == 0)
    def _init():
        st_ref[...] = jnp.zeros_like(st_ref)
        xtail_ref[...] = jnp.zeros_like(xtail_ref)
        hlast_ref[...] = jnp.zeros_like(hlast_ref)

    ub_ref[...] = _rmsnorm(h_ref[...], ln_ref[...]).astype(BF16)

    _interleave_rows(_dot(ub_ref[...], wmain_ref[:, X0:Y0]), xil_ref)
    yg_ref[...] = _dot(ub_ref[...], wmain_ref[:, Y0:])
    x3 = _slabs_value(xil_ref).reshape(seg, SUBLANES, LRU_WIDTH)
    tail = x3[seg - (LRU_CONV - 1):]
    before = _segment_history(tail, xtail_ref[...])
    xtail_ref[...] = tail
    xc = _causal_conv_interleaved(x3, before, cw_ref, cb_ref, slice(None)).reshape(tt, LRU_WIDTH)
    neg_c_softplus = (-LRU_C) * _softplus(-lam_ref[...])
    half = LRU_WIDTH // 2
    jb = seg // len(QKVG_PIECES)
    rb = jb * SUBLANES
    hrun = prun = None
    for blk, (c_lo, c_hi) in enumerate(QKVG_PIECES):
        xcr = xc[blk * rb:(blk + 1) * rb]
        xcb = xcr.astype(BF16)
        ra = jnp.concatenate([_dot(xcb[:, :half], wa_ref[0]), _dot(xcb[:, half:], wa_ref[1])], axis=1)
        rx = jnp.concatenate([_dot(xcb[:, :half], wx_ref[0]), _dot(xcb[:, half:], wx_ref[1])], axis=1)
        r_gate = _sigmoid(ra + ba_ref[...])
        i_gate = _sigmoid(rx + bx_ref[...])
        log_a = r_gate * neg_c_softplus
        a = jnp.exp(log_a)
        th = jnp.tanh(log_a)
        mult = jnp.sqrt((-2.0 * th) / (1.0 - th))
        if blk == 0:
            row = lax.broadcasted_iota(jnp.int32, (rb, LRU_WIDTH), 0)
            mult = jnp.where(jnp.logical_and(row == 0, i == 0), 1.0, mult)
        u3 = (mult * (i_gate * xcr)).reshape(jb, SUBLANES, LRU_WIDTH)
        a3 = a.reshape(jb, SUBLANES, LRU_WIDTH)
        for j in range(jb):
            if hrun is None:
                hrun, prun = u3[0], a3[0]
            else:
                hrun = a3[j] * hrun + u3[j]
                prun = a3[j] * prun
            hs_ref[blk * jb + j] = hrun
            ps_ref[blk * jb + j] = prun
        qkvg_ref[:, c_lo:c_hi] = _dot(ub_ref[...], wmain_ref[:, c_lo:c_hi])
    sub = lax.broadcasted_iota(jnp.int32, (SUBLANES, LRU_WIDTH), 0)
    h_prev = hlast_ref[...]
    cu = jnp.where(sub == 0, hrun + prun * h_prev, hrun)
    ca = prun
    d = 1
    while d < SUBLANES:
        cu = ca * jnp.where(sub >= d, pltpu.roll(cu, d, axis=0), 0.0) + cu
        if 2 * d < SUBLANES:
            ca = ca * jnp.where(sub >= d, pltpu.roll(ca, d, axis=0), 1.0)
        d *= 2
    h_enter = jnp.where(sub == 0, h_prev, pltpu.roll(cu, 1, axis=0))
    hlast_ref[...] = cu[SUBLANES - 1:]
    h2 = (hs_ref[...] + ps_ref[...] * h_enter[None]).reshape(tt, LRU_WIDTH)
    for c in range(LRU_WIDTH // LANES):
        hil_ref[c] = h2[:, c * LANES:(c + 1) * LANES]
    glr = _dot(ub_ref[...], wgate_ref[...])
    for c in range(LRU_WIDTH // LANES):
        for s in range(SUBLANES):
            rows = slice(s * seg, (s + 1) * seg)
            lanes = slice(c * LANES, (c + 1) * LANES)
            y_ref[rows, GLA_WIDTH + c * LANES:GLA_WIDTH + (c + 1) * LANES] = (
                _time_ordered_block(hil_ref, c, s) * _gelu_tanh(yg_ref[rows, lanes])).astype(BF16)
    out_ref[...] = h_ref[...] + _dot(y_ref[:, GLA_WIDTH:], wout_ref[GLA_WIDTH:, :])

    gate = _dot(glr.astype(BF16), w2_ref[...]) + b2_ref[...]
    log_alpha = -_softplus(-gate) * (1.0 / GLA_GATE_NORMALIZER)
    _interleave_rows(log_alpha, lail_ref)
    la3 = _slabs_value(lail_ref).reshape(seg, SUBLANES, QK_COLS)
    nq = QK_COLS // LANES
    for c0 in range(0, seg, GLA_CHUNK):
        run = la3[c0]
        cum = [run]
        for j in range(c0 + 1, c0 + GLA_CHUNK):
            run = run + la3[j]
            cum.append(run)
        for j in range(GLA_CHUNK):
            for c in range(nq):
                lanes = slice(c * LANES, (c + 1) * LANES)
                rows = slice((c0 + j) * SUBLANES, (c0 + j + 1) * SUBLANES)
                bil_ref[c, rows, :] = cum[j][:, lanes]
                bil_ref[nq + c, rows, :] = (run - cum[j])[:, lanes]
    for s in range(SUBLANES):
        rows = slice(s * seg, (s + 1) * seg)
        b = jnp.concatenate([_time_ordered_block(bil_ref, c, s) for c in range(nq)], axis=1)
        bd = jnp.concatenate([_time_ordered_block(bil_ref, nq + c, s) for c in range(nq)], axis=1)
        q = qkvg_ref[rows, Q0:Q0 + QK_COLS]
        k = qkvg_ref[rows, K0:K0 + QK_COLS]
        bnat_ref[rows, :] = b
        qi_ref[rows, :] = ((q * (GLA_DK ** -0.5)) * jnp.exp(b)).astype(BF16)
        ki_ref[rows, :] = (k * jnp.exp(-b)).astype(BF16)
        kd_ref[rows, :] = (k * jnp.exp(bd)).astype(BF16)

    lane_head = lax.broadcasted_iota(jnp.int32, (GLA_CHUNK, QK_COLS), 1) // GLA_DK
    head_masks = [lane_head == hd for hd in range(GLA_HEADS)]
    c_row = lax.broadcasted_iota(jnp.int32, (GLA_HEADS * GLA_CHUNK, GLA_CHUNK), 0) % GLA_CHUNK
    c_col = lax.broadcasted_iota(jnp.int32, (GLA_HEADS * GLA_CHUNK, GLA_CHUNK), 1)
    causal = c_row >= c_col

    def stack_heads(x):
        zero = jnp.zeros_like(x)
        return jnp.concatenate([jnp.where(m, x, zero) for m in head_masks], axis=0)

    gnorm = gnorm_ref[...]
    n_chunks = tt // GLA_CHUNK
    out_group = n_chunks // 2
    for c in range(n_chunks):
        rows = slice(c * GLA_CHUNK, (c + 1) * GLA_CHUNK)
        qs = stack_heads(qi_ref[rows, :])
        ks = stack_heads(kd_ref[rows, :])
        v = qkvg_ref[rows, V0:V0 + GLA_WIDTH].astype(BF16)
        a_all = jnp.where(causal, _dot_nt(qs, ki_ref[rows, :]), 0.0).astype(BF16)
        st = st_ref[...]
        o_inter = _dot_nt(qs, st.astype(BF16))
        for hd in range(GLA_HEADS):
            hr = slice(hd * GLA_CHUNK, (hd + 1) * GLA_CHUNK)
            vs = slice(hd * GLA_DV, (hd + 1) * GLA_DV)
            o = _dot(a_all[hr], v[:, vs]) + o_inter[hr]
            gh = qkvg_ref[rows, G0 + hd * GLA_DV:G0 + (hd + 1) * GLA_DV]
            y_ref[rows, vs] = (_rmsnorm(o, gnorm) * (gh * _sigmoid(gh))).astype(BF16)
        v_stack = jnp.concatenate([v[:, hd * GLA_DV:(hd + 1) * GLA_DV] for hd in range(GLA_HEADS)], axis=0)
        decay = jnp.exp(bnat_ref[(c + 1) * GLA_CHUNK - 1:(c + 1) * GLA_CHUNK, :])
        st_ref[...] = decay * st + _dot_tn(v_stack, ks)
        if (c + 1) % out_group == 0:
            orows = slice((c + 1 - out_group) * GLA_CHUNK, (c + 1) * GLA_CHUNK)
            out_ref[orows, :] = out_ref[orows, :] + _dot(y_ref[orows, :GLA_WIDTH], wout_ref[:GLA_WIDTH, :])


def _ffn_kernel(h_ref, ln_ref, win_ref, cw_ref, cb_ref, wdown_ref, lnf_ref, out_ref,
                hil_ref, oil_ref, tail_ref, act_ref, *, tt, final_norm):
    i = pl.program_id(0)
    seg = tt // SUBLANES
    hist = FFN_CONV - 1

    @pl.when(i == 0)
    def _init():
        tail_ref[...] = jnp.zeros_like(tail_ref)

    _interleave_rows(h_ref, hil_ref)
    h_in = _slabs_value(hil_ref)
    ub = _rmsnorm(h_in, ln_ref[...]).astype(BF16)

    def conv(c0):
        cols = slice(c0, c0 + FFN_COL_BLOCK)
        z = _dot(ub, win_ref[:, cols]).reshape(seg, SUBLANES, FFN_COL_BLOCK)
        tail = z[seg - hist:]
        before = _segment_history(tail, tail_ref[:, :, cols])
        tail_ref[:, :, cols] = tail
        return _causal_conv_interleaved(z, before, cw_ref, cb_ref, cols).reshape(tt, FFN_COL_BLOCK)

    for blk in range(FFN_HIDDEN // FFN_COL_BLOCK):
        c0 = blk * FFN_COL_BLOCK
        za = conv(c0)
        zg = conv(FFN_HIDDEN + c0)
        act_ref[:, c0:c0 + FFN_COL_BLOCK] = (_gelu_tanh(za) * zg).astype(BF16)

    out = h_in + _dot(act_ref[...], wdown_ref[...])
    if final_norm:
        out = _rmsnorm(out, lnf_ref[...])
    n_slabs = D_MODEL // LANES
    for c in range(n_slabs):
        oil_ref[c] = out[:, c * LANES:(c + 1) * LANES]
    for c in range(n_slabs):
        for s in range(SUBLANES):
            out_ref[s * seg:(s + 1) * seg, c * LANES:(c + 1) * LANES] = _time_ordered_block(oil_ref, c, s)


def _const_spec(shape):
    zeros = (0,) * len(shape)
    return pl.BlockSpec(shape, lambda i: zeros, pipeline_mode=pl.Buffered(1))


def _time_tile(t):
    assert t % TIME_TILE == 0
    return TIME_TILE


def _mixer_call(h, ln, wmain, wgate, w2, b2, gnorm, cw, cb, wa, ba, wx, bx, lam, wout):
    t = h.shape[0]
    tt = _time_tile(t)
    seg = tt // SUBLANES
    consts = (ln, wmain, wgate, w2, b2, gnorm, cw, cb, wa, ba, wx, bx, lam, wout)
    row_spec = pl.BlockSpec((tt, D_MODEL), lambda i: (i, 0))
    return pl.pallas_call(
        functools.partial(_mixer_kernel, tt=tt),
        grid=(t // tt,),
        in_specs=[row_spec] + [_const_spec(c.shape) for c in consts],
        out_specs=row_spec,
        out_shape=jax.ShapeDtypeStruct((t, D_MODEL), F32),
        scratch_shapes=[
            pltpu.VMEM((tt, D_MODEL), BF16),
            pltpu.VMEM((tt, X0), F32),
            pltpu.VMEM((tt, LRU_WIDTH), F32),
            pltpu.VMEM((LRU_WIDTH // LANES, tt, LANES), F32),
            pltpu.VMEM((QK_COLS // LANES, tt, LANES), F32),
            pltpu.VMEM((2 * QK_COLS // LANES, tt, LANES), F32),
            pltpu.VMEM((tt, QK_COLS), F32),
            pltpu.VMEM((tt, QK_COLS), BF16),
            pltpu.VMEM((tt, QK_COLS), BF16),
            pltpu.VMEM((tt, QK_COLS), BF16),
            pltpu.VMEM((tt, D_MODEL), BF16),
            pltpu.VMEM((GLA_DV, QK_COLS), F32),
            pltpu.VMEM((LRU_CONV - 1, SUBLANES, LRU_WIDTH), F32),
            pltpu.VMEM((seg, SUBLANES, LRU_WIDTH), F32),
            pltpu.VMEM((seg, SUBLANES, LRU_WIDTH), F32),
            pltpu.VMEM((LRU_WIDTH // LANES, tt, LANES), F32),
            pltpu.VMEM((1, LRU_WIDTH), F32),
        ],
        compiler_params=pltpu.CompilerParams(
            dimension_semantics=("arbitrary",), vmem_limit_bytes=VMEM_LIMIT_BYTES),
        name="mixer",
    )(h, *consts)


def _ffn_call(h, ln, win, cw, cb, wdown, lnf, final_norm):
    t = h.shape[0]
    tt = _time_tile(t)
    consts = (ln, win, cw, cb, wdown, lnf)
    row_spec = pl.BlockSpec((tt, D_MODEL), lambda i: (i, 0))
    return pl.pallas_call(
        functools.partial(_ffn_kernel, tt=tt, final_norm=final_norm),
        grid=(t // tt,),
        in_specs=[row_spec] + [_const_spec(c.shape) for c in consts],
        out_specs=row_spec,
        out_shape=jax.ShapeDtypeStruct((t, D_MODEL), F32),
        scratch_shapes=[
            pltpu.VMEM((D_MODEL // LANES, tt, LANES), F32),
            pltpu.VMEM((D_MODEL // LANES, tt, LANES), F32),
            pltpu.VMEM((FFN_CONV - 1, SUBLANES, 2 * FFN_HIDDEN), F32),
            pltpu.VMEM((tt, FFN_HIDDEN), BF16),
        ],
        compiler_params=pltpu.CompilerParams(
            dimension_semantics=("arbitrary",), vmem_limit_bytes=VMEM_LIMIT_BYTES),
        name="ffn",
    )(h, *consts)


def _block_diag_halves(w):
    per = MXU_DIM // LRU_BLOCK
    tiles = []
    for s in range(LRU_BLOCKS // per):
        tiles.append(jax.scipy.linalg.block_diag(*[w[s * per + j] for j in range(per)]))
    return jnp.stack(tiles).astype(BF16)


def kernel(x, ln_mix, w_in, gla_gate_w2, gla_gate_b, gla_norm, lru_conv_w, lru_conv_b,
           lru_wa, lru_ba, lru_wx, lru_bx, lru_lambda, w_out, ln_ffn, ffn_w_in,
           ffn_conv_w, ffn_conv_b, ffn_w_down, ln_final):
    bsz, t, d = x.shape
    assert bsz == 1 and d == D_MODEL
    depth = w_in.shape[0]
    h = x.reshape(t, d)
    a0 = Q0 + 2 * QK_COLS + 2 * GLA_WIDTH
    a1 = a0 + GLA_GATE_RANK
    row = lambda p: p.reshape(1, -1)
    for l in range(depth):
        wmain = jnp.concatenate([w_in[l][:, :a0], w_in[l][:, a1:]], axis=1).astype(BF16)
        wgate = jnp.pad(w_in[l][:, a0:a1], ((0, 0), (0, GATE_PAD - GLA_GATE_RANK))).astype(BF16)
        w2 = jnp.pad(gla_gate_w2[l], ((0, GATE_PAD - GLA_GATE_RANK), (0, 0))).astype(BF16)
        h = _mixer_call(
            h, row(ln_mix[l]), wmain, wgate, w2, row(gla_gate_b[l]), row(gla_norm[l]),
            lru_conv_w[l], row(lru_conv_b[l]), _block_diag_halves(lru_wa[l]), row(lru_ba[l]),
            _block_diag_halves(lru_wx[l]), row(lru_bx[l]), row(lru_lambda[l]),
            w_out[l].astype(BF16))
        h = _ffn_call(
            h, row(ln_ffn[l]), ffn_w_in[l].astype(BF16), ffn_conv_w[l], row(ffn_conv_b[l]),
            ffn_w_down[l].astype(BF16), row(ln_final), final_norm=(l == depth - 1))
    return h.reshape(bsz, t, d)
```

```python
import functools
import math

import jax
import jax.numpy as jnp
from jax import lax
from jax.experimental import pallas as pl
from jax.experimental.pallas import tpu as pltpu

D_MODEL = 1024
GLA_HEADS = 4
GLA_DV = 128
GLA_DK = 64
GLA_GATE_RANK = 16
GLA_GATE_NORMALIZER = 16.0
GLA_CHUNK = 64
GLA_WIDTH = GLA_HEADS * GLA_DV
QK_COLS = GLA_HEADS * GLA_DK
LRU_WIDTH = D_MODEL - GLA_WIDTH
LRU_BLOCKS = 8
LRU_BLOCK = LRU_WIDTH // LRU_BLOCKS
LRU_CONV = 4
LRU_C = 8.0
FFN_HIDDEN = 3 * D_MODEL
FFN_CONV = 3
EPS = 1e-6

LANES = 128
SUBLANES = 8
MXU_DIM = 256
VMEM_LIMIT_BYTES = 56 * 1024 * 1024

Q0, K0, V0, G0, X0, Y0 = 0, 256, 512, 1024, 1536, 2048
MAIN_COLS = 2560
QKVG_PIECES = ((0, 512), (512, 1024), (1024, 1280), (1280, 1536))
GATE_PAD = LANES
FFN_COL_BLOCK = 512
TIME_TILE = 1024

F32 = jnp.float32
BF16 = jnp.bfloat16


def _dot(a, b):
    return jnp.dot(a, b, preferred_element_type=F32)


def _dot_nt(a, b):
    return lax.dot_general(a, b, (((1,), (1,)), ((), ())), preferred_element_type=F32)


def _dot_tn(a, b):
    return lax.dot_general(a, b, (((0,), (0,)), ((), ())), preferred_element_type=F32)


def _rmsnorm(x, g):
    return x * lax.rsqrt(jnp.mean(x * x, axis=-1, keepdims=True) + EPS) * g


def _softplus(x):
    return jnp.maximum(x, 0.0) + jnp.log1p(jnp.exp(-jnp.abs(x)))


def _sigmoid(x):
    return 1.0 / (1.0 + jnp.exp(-x))


def _gelu_tanh(x):
    c = math.sqrt(2.0 / math.pi)
    hx = 0.5 * x
    return hx + hx * jnp.tanh(x * (c + (c * 0.044715) * (x * x)))


def _interleave_rows(src, slab_ref):
    n, tt, _ = slab_ref.shape
    seg = tt // SUBLANES
    for c in range(n):
        for s in range(SUBLANES):
            slab_ref[c, pl.ds(s, seg, stride=SUBLANES), :] = (
                src[s * seg:(s + 1) * seg, c * LANES:(c + 1) * LANES])


def _time_ordered_block(slab_ref, c, s):
    seg = slab_ref.shape[1] // SUBLANES
    return slab_ref[c, pl.ds(s, seg, stride=SUBLANES), :]


def _slabs_value(slab_ref, lo=0, hi=None):
    hi = slab_ref.shape[0] if hi is None else hi
    return jnp.concatenate([slab_ref[c] for c in range(lo, hi)], axis=1)


def _segment_history(tail, prev_tail):
    first = lax.broadcasted_iota(jnp.int32, tail.shape, 1) == 0
    return jnp.where(first, pltpu.roll(prev_tail, 1, axis=1), pltpu.roll(tail, 1, axis=1))


def _causal_conv_interleaved(x3, before, w_ref, b_ref, cols):
    taps = w_ref.shape[0]
    hist = taps - 1
    seg = x3.shape[0]
    out = b_ref[:, cols] + x3 * w_ref[hist:taps, cols]
    for k in range(1, taps):
        xk = jnp.concatenate([before[hist - k:], x3[:seg - k]], axis=0)
        out = out + xk * w_ref[hist - k:hist - k + 1, cols]
    return out


def _mixer_kernel(h_ref, ln_ref, wmain_ref, wgate_ref, w2_ref, b2_ref, gnorm_ref,
                  cw_ref, cb_ref, wa_ref, ba_ref, wx_ref, bx_ref, lam_ref, wout_ref,
                  out_ref,
                  ub_ref, qkvg_ref, yg_ref, xil_ref, lail_ref, bil_ref, bnat_ref, qi_ref, ki_ref, kd_ref,
                  y_ref, st_ref, xtail_ref, hs_ref, ps_ref, hil_ref, hlast_ref, *, tt):
    i = pl.program_id(0)
    seg = tt // SUBLANES

    @pl.when(i == 0)
    def _init():
        st_ref[...] = jnp.zeros_like(st_ref)
        xtail_ref[...] = jnp.zeros_like(xtail_ref)
        hlast_ref[...] = jnp.zeros_like(hlast_ref)

    ub_ref[...] = _rmsnorm(h_ref[...], ln_ref[...]).astype(BF16)

    _interleave_rows(_dot(ub_ref[...], wmain_ref[:, X0:Y0]), xil_ref)
    yg_ref[...] = _dot(ub_ref[...], wmain_ref[:, Y0:])
    x3 = _slabs_value(xil_ref).reshape(seg, SUBLANES, LRU_WIDTH)
    tail = x3[seg - (LRU_CONV - 1):]
    before = _segment_history(tail, xtail_ref[...])
    xtail_ref[...] = tail
    xc = _causal_conv_interleaved(x3, before, cw_ref, cb_ref, slice(None)).reshape(tt, LRU_WIDTH)
    neg_c_softplus = (-LRU_C) * _softplus(-lam_ref[...])
    half = LRU_WIDTH // 2
    jb = seg // len(QKVG_PIECES)
    rb = jb * SUBLANES
    hrun = prun = None
    for blk, (c_lo, c_hi) in enumerate(QKVG_PIECES):
        xcr = xc[blk * rb:(blk + 1) * rb]
        xcb = xcr.astype(BF16)
        ra = jnp.concatenate([_dot(xcb[:, :half], wa_ref[0]), _dot(xcb[:, half:], wa_ref[1])], axis=1)
        rx = jnp.concatenate([_dot(xcb[:, :half], wx_ref[0]), _dot(xcb[:, half:], wx_ref[1])], axis=1)
        r_gate = _sigmoid(ra + ba_ref[...])
        i_gate = _sigmoid(rx + bx_ref[...])
        log_a = r_gate * neg_c_softplus
        a = jnp.exp(log_a)
        th = jnp.tanh(log_a)
        mult = jnp.sqrt((-2.0 * th) / (1.0 - th))
        if blk == 0:
            row = lax.broadcasted_iota(jnp.int32, (rb, LRU_WIDTH), 0)
            mult = jnp.where(jnp.logical_and(row == 0, i == 0), 1.0, mult)
        u3 = (mult * (i_gate * xcr)).reshape(jb, SUBLANES, LRU_WIDTH)
        a3 = a.reshape(jb, SUBLANES, LRU_WIDTH)
        for j in range(jb):
            if hrun is None:
                hrun, prun = u3[0], a3[0]
            else:
                hrun = a3[j] * hrun + u3[j]
                prun = a3[j] * prun
            hs_ref[blk * jb + j] = hrun
            ps_ref[blk * jb + j] = prun
        qkvg_ref[:, c_lo:c_hi] = _dot(ub_ref[...], wmain_ref[:, c_lo:c_hi])
    sub = lax.broadcasted_iota(jnp.int32, (SUBLANES, LRU_WIDTH), 0)
    h_prev = hlast_ref[...]
    cu = jnp.where(sub == 0, hrun + prun * h_prev, hrun)
    ca = prun
    d = 1
    while d < SUBLANES:
        cu = ca * jnp.where(sub >= d, pltpu.roll(cu, d, axis=0), 0.0) + cu
        if 2 * d < SUBLANES:
            ca = ca * jnp.where(sub >= d, pltpu.roll(ca, d, axis=0), 1.0)
        d *= 2
    h_enter = jnp.where(sub == 0, h_prev, pltpu.roll(cu, 1, axis=0))
    hlast_ref[...] = cu[SUBLANES - 1:]
    h2 = (hs_ref[...] + ps_ref[...] * h_enter[None]).reshape(tt, LRU_WIDTH)
    for c in range(LRU_WIDTH // LANES):
        hil_ref[c] = h2[:, c * LANES:(c + 1) * LANES]
    glr = _dot(ub_ref[...], wgate_ref[...])
    for c in range(LRU_WIDTH // LANES):
        for s in range(SUBLANES):
            rows = slice(s * seg, (s + 1) * seg)
            lanes = slice(c * LANES, (c + 1) * LANES)
            y_ref[rows, GLA_WIDTH + c * LANES:GLA_WIDTH + (c + 1) * LANES] = (
                _time_ordered_block(hil_ref, c, s) * _gelu_tanh(yg_ref[rows, lanes])).astype(BF16)
    out_ref[...] = h_ref[...] + _dot(y_ref[:, GLA_WIDTH:], wout_ref[GLA_WIDTH:, :])

    gate = _dot(glr.astype(BF16), w2_ref[...]) + b2_ref[...]
    log_alpha = -_softplus(-gate) * (1.0 / GLA_GATE_NORMALIZER)
    _interleave_rows(log_alpha, lail_ref)
    la3 = _slabs_value(lail_ref).reshape(seg, SUBLANES, QK_COLS)
    nq = QK_COLS // LANES
    for c0 in range(0, seg, GLA_CHUNK):
        run = la3[c0]
        cum = [run]
        for j in range(c0 + 1, c0 + GLA_CHUNK):
            run = run + la3[j]
            cum.append(run)
        for j in range(GLA_CHUNK):
            for c in range(nq):
                lanes = slice(c * LANES, (c + 1) * LANES)
                rows = slice((c0 + j) * SUBLANES, (c0 + j + 1) * SUBLANES)
                bil_ref[c, rows, :] = cum[j][:, lanes]
                bil_ref[nq + c, rows, :] = (run - cum[j])[:, lanes]
    for s in range(SUBLANES):
        rows = slice(s * seg, (s + 1) * seg)
        b = jnp.concatenate([_time_ordered_block(bil_ref, c, s) for c in range(nq)], axis=1)
        bd = jnp.concatenate([_time_ordered_block(bil_ref, nq + c, s) for c in range(nq)], axis=1)
        q = qkvg_ref[rows, Q0:Q0 + QK_COLS]
        k = qkvg_ref[rows, K0:K0 + QK_COLS]
        bnat_ref[rows, :] = b
        qi_ref[rows, :] = ((q * (GLA_DK ** -0.5)) * jnp.exp(b)).astype(BF16)
        ki_ref[rows, :] = (k * jnp.exp(-b)).astype(BF16)
        kd_ref[rows, :] = (k * jnp.exp(bd)).astype(BF16)

    lane_head = lax.broadcasted_iota(jnp.int32, (GLA_CHUNK, QK_COLS), 1) // GLA_DK
    head_masks = [lane_head == hd for hd in range(GLA_HEADS)]
    c_row = lax.broadcasted_iota(jnp.int32, (GLA_HEADS * GLA_CHUNK, GLA_CHUNK), 0) % GLA_CHUNK
    c_col = lax.broadcasted_iota(jnp.int32, (GLA_HEADS * GLA_CHUNK, GLA_CHUNK), 1)
    causal = c_row >= c_col

    def stack_heads(x):
        zero = jnp.zeros_like(x)
        return jnp.concatenate([jnp.where(m, x, zero) for m in head_masks], axis=0)

    gnorm = gnorm_ref[...]
    n_chunks = tt // GLA_CHUNK
    out_group = n_chunks // 2
    for c in range(n_chunks):
        rows = slice(c * GLA_CHUNK, (c + 1) * GLA_CHUNK)
        qs = stack_heads(qi_ref[rows, :])
        ks = stack_heads(kd_ref[rows, :])
        v = qkvg_ref[rows, V0:V0 + GLA_WIDTH].astype(BF16)
        a_all = jnp.where(causal, _dot_nt(qs, ki_ref[rows, :]), 0.0).astype(BF16)
        st = st_ref[...]
        o_inter = _dot_nt(qs, st.astype(BF16))
        for hd in range(GLA_HEADS):
            hr = slice(hd * GLA_CHUNK, (hd + 1) * GLA_CHUNK)
            vs = slice(hd * GLA_DV, (hd + 1) * GLA_DV)
            o = _dot(a_all[hr], v[:, vs]) + o_inter[hr]
            gh = qkvg_ref[rows, G0 + hd * GLA_DV:G0 + (hd + 1) * GLA_DV]
            y_ref[rows, vs] = (_rmsnorm(o, gnorm) * (gh * _sigmoid(gh))).astype(BF16)
        v_stack = jnp.concatenate([v[:, hd * GLA_DV:(hd + 1) * GLA_DV] for hd in range(GLA_HEADS)], axis=0)
        decay = jnp.exp(bnat_ref[(c + 1) * GLA_CHUNK - 1:(c + 1) * GLA_CHUNK, :])
        st_ref[...] = decay * st + _dot_tn(v_stack, ks)
        if (c + 1) % out_group == 0:
            orows = slice((c + 1 - out_group) * GLA_CHUNK, (c + 1) * GLA_CHUNK)
            out_ref[orows, :] = out_ref[orows, :] + _dot(y_ref[orows, :GLA_WIDTH], wout_ref[:GLA_WIDTH, :])


def _ffn_kernel(h_ref, ln_ref, win_ref, cw_ref, cb_ref, wdown_ref, lnf_ref, out_ref,
                hil_ref, oil_ref, tail_ref, act_ref, *, tt, final_norm):
    i = pl.program_id(0)
    seg = tt // SUBLANES
    hist = FFN_CONV - 1

    @pl.when(i == 0)
    def _init():
        tail_ref[...] = jnp.zeros_like(tail_ref)

    _interleave_rows(h_ref, hil_ref)
    h_in = _slabs_value(hil_ref)
    ub = _rmsnorm(h_in, ln_ref[...]).astype(BF16)

    def conv(c0):
        cols = slice(c0, c0 + FFN_COL_BLOCK)
        z = _dot(ub, win_ref[:, cols]).reshape(seg, SUBLANES, FFN_COL_BLOCK)
        tail = z[seg - hist:]
        before = _segment_history(tail, tail_ref[:, :, cols])
        tail_ref[:, :, cols] = tail
        return _causal_conv_interleaved(z, before, cw_ref, cb_ref, cols).reshape(tt, FFN_COL_BLOCK)

    for blk in range(FFN_HIDDEN // FFN_COL_BLOCK):
        c0 = blk * FFN_COL_BLOCK
        za = conv(c0)
        zg = conv(FFN_HIDDEN + c0)
        act_ref[:, c0:c0 + FFN_COL_BLOCK] = (_gelu_tanh(za) * zg).astype(BF16)

    out = h_in + _dot(act_ref[...], wdown_ref[...])
    if final_norm:
        out = _rmsnorm(out, lnf_ref[...])
    n_slabs = D_MODEL // LANES
    for c in range(n_slabs):
        oil_ref[c] = out[:, c * LANES:(c + 1) * LANES]
    for c in range(n_slabs):
        for s in range(SUBLANES):
            out_ref[s * seg:(s + 1) * seg, c * LANES:(c + 1) * LANES] = _time_ordered_block(oil_ref, c, s)


def _const_spec(shape):
    zeros = (0,) * len(shape)
    return pl.BlockSpec(shape, lambda i: zeros, pipeline_mode=pl.Buffered(1))


def _time_tile(t):
    assert t % TIME_TILE == 0 and (TIME_TILE // SUBLANES) % GLA_CHUNK == 0
    return TIME_TILE


def _mixer_call(h, ln, wmain, wgate, w2, b2, gnorm, cw, cb, wa, ba, wx, bx, lam, wout):
    t = h.shape[0]
    tt = _time_tile(t)
    seg = tt // SUBLANES
    consts = (ln, wmain, wgate, w2, b2, gnorm, cw, cb, wa, ba, wx, bx, lam, wout)
    row_spec = pl.BlockSpec((tt, D_MODEL), lambda i: (i, 0))
    return pl.pallas_call(
        functools.partial(_mixer_kernel, tt=tt),
        grid=(t // tt,),
        in_specs=[row_spec] + [_const_spec(c.shape) for c in consts],
        out_specs=row_spec,
        out_shape=jax.ShapeDtypeStruct((t, D_MODEL), F32),
        scratch_shapes=[
            pltpu.VMEM((tt, D_MODEL), BF16),
            pltpu.VMEM((tt, X0), F32),
            pltpu.VMEM((tt, LRU_WIDTH), F32),
            pltpu.VMEM((LRU_WIDTH // LANES, tt, LANES), F32),
            pltpu.VMEM((QK_COLS // LANES, tt, LANES), F32),
            pltpu.VMEM((2 * QK_COLS // LANES, tt, LANES), F32),
            pltpu.VMEM((tt, QK_COLS), F32),
            pltpu.VMEM((tt, QK_COLS), BF16),
            pltpu.VMEM((tt, QK_COLS), BF16),
            pltpu.VMEM((tt, QK_COLS), BF16),
            pltpu.VMEM((tt, D_MODEL), BF16),
            pltpu.VMEM((GLA_DV, QK_COLS), F32),
            pltpu.VMEM((LRU_CONV - 1, SUBLANES, LRU_WIDTH), F32),
            pltpu.VMEM((seg, SUBLANES, LRU_WIDTH), F32),
            pltpu.VMEM((seg, SUBLANES, LRU_WIDTH), F32),
            pltpu.VMEM((LRU_WIDTH // LANES, tt, LANES), F32),
            pltpu.VMEM((1, LRU_WIDTH), F32),
        ],
        compiler_params=pltpu.CompilerParams(
            dimension_semantics=("arbitrary",), vmem_limit_bytes=VMEM_LIMIT_BYTES),
        name="mixer",
    )(h, *consts)


def _ffn_call(h, ln, win, cw, cb, wdown, lnf, final_norm):
    t = h.shape[0]
    tt = _time_tile(t)
    consts = (ln, win, cw, cb, wdown, lnf)
    row_spec = pl.BlockSpec((tt, D_MODEL), lambda i: (i, 0))
    return pl.pallas_call(
        functools.partial(_ffn_kernel, tt=tt, final_norm=final_norm),
        grid=(t // tt,),
        in_specs=[row_spec] + [_const_spec(c.shape) for c in consts],
        out_specs=row_spec,
        out_shape=jax.ShapeDtypeStruct((t, D_MODEL), F32),
        scratch_shapes=[
            pltpu.VMEM((D_MODEL // LANES, tt, LANES), F32),
            pltpu.VMEM((D_MODEL // LANES, tt, LANES), F32),
            pltpu.VMEM((FFN_CONV - 1, SUBLANES, 2 * FFN_HIDDEN), F32),
            pltpu.VMEM((tt, FFN_HIDDEN), BF16),
        ],
        compiler_params=pltpu.CompilerParams(
            dimension_semantics=("arbitrary",), vmem_limit_bytes=VMEM_LIMIT_BYTES),
        name="ffn",
    )(h, *consts)


def _block_diag_halves(w):
    per = MXU_DIM // LRU_BLOCK
    tiles = []
    for s in range(LRU_BLOCKS // per):
        tiles.append(jax.scipy.linalg.block_diag(*[w[s * per + j] for j in range(per)]))
    return jnp.stack(tiles).astype(BF16)


def kernel(x, ln_mix, w_in, gla_gate_w2, gla_gate_b, gla_norm, lru_conv_w, lru_conv_b,
           lru_wa, lru_ba, lru_wx, lru_bx, lru_lambda, w_out, ln_ffn, ffn_w_in,
           ffn_conv_w, ffn_conv_b, ffn_w_down, ln_final):
    bsz, t, d = x.shape
    assert bsz == 1 and d == D_MODEL
    depth = w_in.shape[0]
    h = x.reshape(t, d)
    a0 = Q0 + 2 * QK_COLS + 2 * GLA_WIDTH
    a1 = a0 + GLA_GATE_RANK
    row = lambda p: p.reshape(1, -1)
    for l in range(depth):
        wmain = jnp.concatenate([w_in[l][:, :a0], w_in[l][:, a1:]], axis=1).astype(BF16)
        wgate = jnp.pad(w_in[l][:, a0:a1], ((0, 0), (0, GATE_PAD - GLA_GATE_RANK))).astype(BF16)
        w2 = jnp.pad(gla_gate_w2[l], ((0, GATE_PAD - GLA_GATE_RANK), (0, 0))).astype(BF16)
        h = _mixer_call(
            h, row(ln_mix[l]), wmain, wgate, w2, row(gla_gate_b[l]), row(gla_norm[l]),
            lru_conv_w[l], row(lru_conv_b[l]), _block_diag_halves(lru_wa[l]), row(lru_ba[l]),
            _block_diag_halves(lru_wx[l]), row(lru_bx[l]), row(lru_lambda[l]),
            w_out[l].astype(BF16))
        h = _ffn_call(
            h, row(ln_ffn[l]), ffn_w_in[l].astype(BF16), ffn_conv_w[l], row(ffn_conv_b[l]),
            ffn_w_down[l].astype(BF16), row(ln_final), final_norm=(l == depth - 1))
    return h.reshape(bsz, t, d)
```

```python
import functools
import math

import jax
import jax.numpy as jnp
from jax import lax
from jax.experimental import pallas as pl
from jax.experimental.pallas import tpu as pltpu

D_MODEL = 1024
GLA_HEADS = 4
GLA_DV = 128
GLA_DK = 64
GLA_GATE_RANK = 16
GLA_GATE_NORMALIZER = 16.0
GLA_CHUNK = 64
GLA_WIDTH = GLA_HEADS * GLA_DV
QK_COLS = GLA_HEADS * GLA_DK
LRU_WIDTH = D_MODEL - GLA_WIDTH
LRU_BLOCKS = 8
LRU_BLOCK = LRU_WIDTH // LRU_BLOCKS
LRU_CONV = 4
LRU_C = 8.0
FFN_HIDDEN = 3 * D_MODEL
FFN_CONV = 3
EPS = 1e-6

LANES = 128
SUBLANES = 8
MXU_DIM = 256
VMEM_LIMIT_BYTES = 56 * 1024 * 1024

Q0, K0, V0, G0, X0 = 0, 256, 512, 1024, 1536
GATE0 = X0
D_IN = GATE0 + GLA_GATE_RANK + 2 * LRU_WIDTH
QKVG_PIECES = ((0, 512), (512, 1024), (1024, 1280), (1280, 1536))
GATE_PAD = LANES
FFN_COL_BLOCK = 512
TIME_TILE = 1024

F32 = jnp.float32
BF16 = jnp.bfloat16


def _dot(a, b):
    return jnp.dot(a, b, preferred_element_type=F32)


def _dot_nt(a, b):
    return lax.dot_general(a, b, (((1,), (1,)), ((), ())), preferred_element_type=F32)


def _dot_tn(a, b):
    return lax.dot_general(a, b, (((0,), (0,)), ((), ())), preferred_element_type=F32)


def _rmsnorm(x, g):
    return x * lax.rsqrt(jnp.mean(x * x, axis=-1, keepdims=True) + EPS) * g


def _softplus(x):
    return jnp.maximum(x, 0.0) + jnp.log1p(jnp.exp(-jnp.abs(x)))


def _sigmoid(x):
    return 1.0 / (1.0 + jnp.exp(-x))


def _gelu_tanh(x):
    c = math.sqrt(2.0 / math.pi)
    hx = 0.5 * x
    return hx + hx * jnp.tanh(x * (c + (c * 0.044715) * (x * x)))


def _interleave_rows(src, slab_ref):
    n, tt, _ = slab_ref.shape
    seg = tt // SUBLANES
    for c in range(n):
        for s in range(SUBLANES):
            slab_ref[c, pl.ds(s, seg, stride=SUBLANES), :] = (
                src[s * seg:(s + 1) * seg, c * LANES:(c + 1) * LANES])


def _time_ordered_block(slab_ref, c, s):
    seg = slab_ref.shape[1] // SUBLANES
    return slab_ref[c, pl.ds(s, seg, stride=SUBLANES), :]


def _slabs_value(slab_ref, lo=0, hi=None):
    hi = slab_ref.shape[0] if hi is None else hi
    return jnp.concatenate([slab_ref[c] for c in range(lo, hi)], axis=1)


def _segment_history(tail, prev_tail):
    first = lax.broadcasted_iota(jnp.int32, tail.shape, 1) == 0
    return jnp.where(first, pltpu.roll(prev_tail, 1, axis=1), pltpu.roll(tail, 1, axis=1))


def _causal_conv_interleaved(x3, before, w_ref, b_ref, cols):
    taps = w_ref.shape[0]
    hist = taps - 1
    seg = x3.shape[0]
    out = b_ref[:, cols] + x3 * w_ref[hist:taps, cols]
    for k in range(1, taps):
        xk = jnp.concatenate([before[hist - k:], x3[:seg - k]], axis=0)
        out = out + xk * w_ref[hist - k:hist - k + 1, cols]
    return out


def _mixer_kernel(h_ref, ln_ref, wqkvg_ref, wxy_ref, wgate_ref, w2_ref, b2_ref, gnorm_ref,
                  cw_ref, cb_ref, wa_ref, ba_ref, wx_ref, bx_ref, lam_ref, wout_ref,
                  *refs, tt, cast_next):
    n_in, n_out = (4, 6) if cast_next else (2, 2)
    cast_in, out_ref, cast_out, scratch = refs[:n_in], refs[n_in], refs[n_in + 1:n_in + 1 + n_out], refs[n_in + 1 + n_out:]
    _cast_weight_rows(cast_in, cast_out)
    _mixer_body(h_ref, ln_ref, wqkvg_ref, wxy_ref, wgate_ref, w2_ref, b2_ref, gnorm_ref,
                cw_ref, cb_ref, wa_ref, ba_ref, wx_ref, bx_ref, lam_ref, wout_ref, out_ref, *scratch, tt=tt)


def _cast_weight_rows(cast_in, cast_out):
    cast_out[0][...] = cast_in[0][...].astype(BF16)
    cast_out[1][...] = cast_in[1][...].astype(BF16)
    if len(cast_in) > 2:
        w = cast_in[2][...]
        wqkvg_n, wxy_n, wgate_n, wout_n = cast_out[2:]
        wqkvg_n[...] = w[:, :GATE0].astype(BF16)
        wxy_n[...] = w[:, GATE0 + GLA_GATE_RANK:].astype(BF16)
        lane = lax.broadcasted_iota(jnp.int32, (w.shape[0], GATE_PAD), 1)
        wgate_n[...] = jnp.where(lane < GLA_GATE_RANK, w[:, GATE0:GATE0 + GATE_PAD], 0.0).astype(BF16)
        wout_n[...] = cast_in[3][...].astype(BF16)


def _mixer_body(h_ref, ln_ref, wqkvg_ref, wxy_ref, wgate_ref, w2_ref, b2_ref, gnorm_ref,
                cw_ref, cb_ref, wa_ref, ba_ref, wx_ref, bx_ref, lam_ref, wout_ref,
                out_ref,
                ub_ref, qkvg_ref, yg_ref, xil_ref, lail_ref, bil_ref, bnat_ref, qi_ref, ki_ref, kd_ref,
                  y_ref, st_ref, xtail_ref, hs_ref, ps_ref, hil_ref, hlast_ref, *, tt):
    i = pl.program_id(0)
    seg = tt // SUBLANES

    @pl.when(i == 0)
    def _init():
        st_ref[...] = jnp.zeros_like(st_ref)
        xtail_ref[...] = jnp.zeros_like(xtail_ref)
        hlast_ref[...] = jnp.zeros_like(hlast_ref)

    ub_ref[...] = _rmsnorm(h_ref[...], ln_ref[...]).astype(BF16)

    _interleave_rows(_dot(ub_ref[...], wxy_ref[:, :LRU_WIDTH]), xil_ref)
    yg_ref[...] = _dot(ub_ref[...], wxy_ref[:, LRU_WIDTH:])
    x3 = _slabs_value(xil_ref).reshape(seg, SUBLANES, LRU_WIDTH)
    tail = x3[seg - (LRU_CONV - 1):]
    before = _segment_history(tail, xtail_ref[...])
    xtail_ref[...] = tail
    xc = _causal_conv_interleaved(x3, before, cw_ref, cb_ref, slice(None)).reshape(tt, LRU_WIDTH)
    neg_c_softplus = (-LRU_C) * _softplus(-lam_ref[...])
    half = LRU_WIDTH // 2
    jb = seg // len(QKVG_PIECES)
    rb = jb * SUBLANES
    hrun = prun = None
    for blk, (c_lo, c_hi) in enumerate(QKVG_PIECES):
        xcr = xc[blk * rb:(blk + 1) * rb]
        xcb = xcr.astype(BF16)
        ra = jnp.concatenate([_dot(xcb[:, :half], wa_ref[0]), _dot(xcb[:, half:], wa_ref[1])], axis=1)
        rx = jnp.concatenate([_dot(xcb[:, :half], wx_ref[0]), _dot(xcb[:, half:], wx_ref[1])], axis=1)
        r_gate = _sigmoid(ra + ba_ref[...])
        i_gate = _sigmoid(rx + bx_ref[...])
        log_a = r_gate * neg_c_softplus
        a = jnp.exp(log_a)
        th = jnp.tanh(log_a)
        mult = jnp.sqrt((-2.0 * th) / (1.0 - th))
        if blk == 0:
            row = lax.broadcasted_iota(jnp.int32, (rb, LRU_WIDTH), 0)
            mult = jnp.where(jnp.logical_and(row == 0, i == 0), 1.0, mult)
        u3 = (mult * (i_gate * xcr)).reshape(jb, SUBLANES, LRU_WIDTH)
        a3 = a.reshape(jb, SUBLANES, LRU_WIDTH)
        for j in range(jb):
            if hrun is None:
                hrun, prun = u3[0], a3[0]
            else:
                hrun = a3[j] * hrun + u3[j]
                prun = a3[j] * prun
            hs_ref[blk * jb + j] = hrun
            ps_ref[blk * jb + j] = prun
        qkvg_ref[:, c_lo:c_hi] = _dot(ub_ref[...], wqkvg_ref[:, c_lo:c_hi])
    sub = lax.broadcasted_iota(jnp.int32, (SUBLANES, LRU_WIDTH), 0)
    h_prev = hlast_ref[...]
    cu = jnp.where(sub == 0, hrun + prun * h_prev, hrun)
    ca = prun
    d = 1
    while d < SUBLANES:
        cu = ca * jnp.where(sub >= d, pltpu.roll(cu, d, axis=0), 0.0) + cu
        if 2 * d < SUBLANES:
            ca = ca * jnp.where(sub >= d, pltpu.roll(ca, d, axis=0), 1.0)
        d *= 2
    h_enter = jnp.where(sub == 0, h_prev, pltpu.roll(cu, 1, axis=0))
    hlast_ref[...] = cu[SUBLANES - 1:]
    h2 = (hs_ref[...] + ps_ref[...] * h_enter[None]).reshape(tt, LRU_WIDTH)
    for c in range(LRU_WIDTH // LANES):
        hil_ref[c] = h2[:, c * LANES:(c + 1) * LANES]
    glr = _dot(ub_ref[...], wgate_ref[...])
    for c in range(LRU_WIDTH // LANES):
        for s in range(SUBLANES):
            rows = slice(s * seg, (s + 1) * seg)
            lanes = slice(c * LANES, (c + 1) * LANES)
            y_ref[rows, GLA_WIDTH + c * LANES:GLA_WIDTH + (c + 1) * LANES] = (
                _time_ordered_block(hil_ref, c, s) * _gelu_tanh(yg_ref[rows, lanes])).astype(BF16)
    out_ref[...] = h_ref[...] + _dot(y_ref[:, GLA_WIDTH:], wout_ref[GLA_WIDTH:, :])

    gate = _dot(glr.astype(BF16), w2_ref[...]) + b2_ref[...]
    log_alpha = -_softplus(-gate) * (1.0 / GLA_GATE_NORMALIZER)
    _interleave_rows(log_alpha, lail_ref)
    la3 = _slabs_value(lail_ref).reshape(seg, SUBLANES, QK_COLS)
    nq = QK_COLS // LANES
    for c0 in range(0, seg, GLA_CHUNK):
        run = la3[c0]
        cum = [run]
        for j in range(c0 + 1, c0 + GLA_CHUNK):
            run = run + la3[j]
            cum.append(run)
        for j in range(GLA_CHUNK):
            for c in range(nq):
                lanes = slice(c * LANES, (c + 1) * LANES)
                rows = slice((c0 + j) * SUBLANES, (c0 + j + 1) * SUBLANES)
                bil_ref[c, rows, :] = cum[j][:, lanes]
                bil_ref[nq + c, rows, :] = (run - cum[j])[:, lanes]
    for s in range(SUBLANES):
        rows = slice(s * seg, (s + 1) * seg)
        b = jnp.concatenate([_time_ordered_block(bil_ref, c, s) for c in range(nq)], axis=1)
        bd = jnp.concatenate([_time_ordered_block(bil_ref, nq + c, s) for c in range(nq)], axis=1)
        q = qkvg_ref[rows, Q0:Q0 + QK_COLS]
        k = qkvg_ref[rows, K0:K0 + QK_COLS]
        bnat_ref[rows, :] = b
        qi_ref[rows, :] = ((q * (GLA_DK ** -0.5)) * jnp.exp(b)).astype(BF16)
        ki_ref[rows, :] = (k * jnp.exp(-b)).astype(BF16)
        kd_ref[rows, :] = (k * jnp.exp(bd)).astype(BF16)

    lane_head = lax.broadcasted_iota(jnp.int32, (GLA_CHUNK, QK_COLS), 1) // GLA_DK
    head_masks = [lane_head == hd for hd in range(GLA_HEADS)]
    c_row = lax.broadcasted_iota(jnp.int32, (GLA_HEADS * GLA_CHUNK, GLA_CHUNK), 0) % GLA_CHUNK
    c_col = lax.broadcasted_iota(jnp.int32, (GLA_HEADS * GLA_CHUNK, GLA_CHUNK), 1)
    causal = c_row >= c_col

    def stack_heads(x):
        zero = jnp.zeros_like(x)
        return jnp.concatenate([jnp.where(m, x, zero) for m in head_masks], axis=0)

    gnorm = gnorm_ref[...]
    n_chunks = tt // GLA_CHUNK
    out_group = n_chunks // 2
    for c in range(n_chunks):
        rows = slice(c * GLA_CHUNK, (c + 1) * GLA_CHUNK)
        qs = stack_heads(qi_ref[rows, :])
        ks = stack_heads(kd_ref[rows, :])
        v = qkvg_ref[rows, V0:V0 + GLA_WIDTH].astype(BF16)
        a_all = jnp.where(causal, _dot_nt(qs, ki_ref[rows, :]), 0.0).astype(BF16)
        st = st_ref[...]
        o_inter = _dot_nt(qs, st.astype(BF16))
        for hd in range(GLA_HEADS):
            hr = slice(hd * GLA_CHUNK, (hd + 1) * GLA_CHUNK)
            vs = slice(hd * GLA_DV, (hd + 1) * GLA_DV)
            o = _dot(a_all[hr], v[:, vs]) + o_inter[hr]
            gh = qkvg_ref[rows, G0 + hd * GLA_DV:G0 + (hd + 1) * GLA_DV]
            y_ref[rows, vs] = (_rmsnorm(o, gnorm) * (gh * _sigmoid(gh))).astype(BF16)
        v_stack = jnp.concatenate([v[:, hd * GLA_DV:(hd + 1) * GLA_DV] for hd in range(GLA_HEADS)], axis=0)
        decay = jnp.exp(bnat_ref[(c + 1) * GLA_CHUNK - 1:(c + 1) * GLA_CHUNK, :])
        st_ref[...] = decay * st + _dot_tn(v_stack, ks)
        if (c + 1) % out_group == 0:
            orows = slice((c + 1 - out_group) * GLA_CHUNK, (c + 1) * GLA_CHUNK)
            out_ref[orows, :] = out_ref[orows, :] + _dot(y_ref[orows, :GLA_WIDTH], wout_ref[:GLA_WIDTH, :])


def _ffn_kernel(h_ref, ln_ref, win_ref, cw_ref, cb_ref, wdown_ref, lnf_ref, out_ref,
                hil_ref, oil_ref, tail_ref, act_ref, *, tt, final_norm):
    i = pl.program_id(0)
    seg = tt // SUBLANES
    hist = FFN_CONV - 1

    @pl.when(i == 0)
    def _init():
        tail_ref[...] = jnp.zeros_like(tail_ref)

    _interleave_rows(h_ref, hil_ref)
    h_in = _slabs_value(hil_ref)
    ub = _rmsnorm(h_in, ln_ref[...]).astype(BF16)

    def conv(c0):
        cols = slice(c0, c0 + FFN_COL_BLOCK)
        z = _dot(ub, win_ref[:, cols]).reshape(seg, SUBLANES, FFN_COL_BLOCK)
        tail = z[seg - hist:]
        before = _segment_history(tail, tail_ref[:, :, cols])
        tail_ref[:, :, cols] = tail
        return _causal_conv_interleaved(z, before, cw_ref, cb_ref, cols).reshape(tt, FFN_COL_BLOCK)

    for blk in range(FFN_HIDDEN // FFN_COL_BLOCK):
        c0 = blk * FFN_COL_BLOCK
        za = conv(c0)
        zg = conv(FFN_HIDDEN + c0)
        act_ref[:, c0:c0 + FFN_COL_BLOCK] = (_gelu_tanh(za) * zg).astype(BF16)

    out = h_in + _dot(act_ref[...], wdown_ref[...])
    if final_norm:
        out = _rmsnorm(out, lnf_ref[...])
    n_slabs = D_MODEL // LANES
    for c in range(n_slabs):
        oil_ref[c] = out[:, c * LANES:(c + 1) * LANES]
    for c in range(n_slabs):
        for s in range(SUBLANES):
            out_ref[s * seg:(s + 1) * seg, c * LANES:(c + 1) * LANES] = _time_ordered_block(oil_ref, c, s)


def _const_spec(shape):
    zeros = (0,) * len(shape)
    return pl.BlockSpec(shape, lambda i: zeros, pipeline_mode=pl.Buffered(1))


def _layer_spec(arr, layer):
    zeros = (0,) * (arr.ndim - 1)
    return pl.BlockSpec((None,) + arr.shape[1:], lambda i: (layer,) + zeros, pipeline_mode=pl.Buffered(1))


def _layer_rows_spec(arr, layer, n_steps):
    rows = arr.shape[1] // n_steps
    assert rows * n_steps == arr.shape[1] and rows % 16 == 0
    return pl.BlockSpec((None, rows, arr.shape[2]), lambda i: (layer, i, 0))


def _rows_out(n_rows, n_cols, n_steps):
    rows = n_rows // n_steps
    return (jax.ShapeDtypeStruct((n_rows, n_cols), BF16), pl.BlockSpec((rows, n_cols), lambda i: (i, 0)))


def _time_tile(t):
    assert t % TIME_TILE == 0 and (TIME_TILE // SUBLANES) % GLA_CHUNK == 0
    return TIME_TILE


def _mixer_call(h, layer, mixer_w, stacked, raw):
    t = h.shape[0]
    tt = _time_tile(t)
    seg = tt // SUBLANES
    n_steps = t // tt
    depth = raw[0].shape[0]
    cast_next = layer + 1 < depth
    wqkvg, wxy, wgate, wout = mixer_w
    ln, w2, b2, gnorm, cw, cb, wa, ba, wx, bx, lam = stacked
    ffn_w_in, ffn_w_down, w_in, w_out = raw
    row_spec = pl.BlockSpec((tt, D_MODEL), lambda i: (i, 0))
    lspec = lambda a: _layer_spec(a, layer)
    in_specs = [row_spec, lspec(ln), _const_spec(wqkvg.shape), _const_spec(wxy.shape), _const_spec(wgate.shape),
                lspec(w2), lspec(b2), lspec(gnorm), lspec(cw), lspec(cb), lspec(wa), lspec(ba), lspec(wx),
                lspec(bx), lspec(lam), _const_spec(wout.shape),
                _layer_rows_spec(ffn_w_in, layer, n_steps), _layer_rows_spec(ffn_w_down, layer, n_steps)]
    args = [h, ln, wqkvg, wxy, wgate, w2, b2, gnorm, cw, cb, wa, ba, wx, bx, lam, wout, ffn_w_in, ffn_w_down]
    outs = [(jax.ShapeDtypeStruct((t, D_MODEL), F32), row_spec),
            _rows_out(D_MODEL, 2 * FFN_HIDDEN, n_steps), _rows_out(FFN_HIDDEN, D_MODEL, n_steps)]
    if cast_next:
        in_specs += [_layer_rows_spec(w_in, layer + 1, n_steps), _layer_rows_spec(w_out, layer + 1, n_steps)]
        args += [w_in, w_out]
        outs += [_rows_out(D_MODEL, GATE0, n_steps), _rows_out(D_MODEL, 2 * LRU_WIDTH, n_steps),
                 _rows_out(D_MODEL, GATE_PAD, n_steps), _rows_out(D_MODEL, D_MODEL, n_steps)]
    res = pl.pallas_call(
        functools.partial(_mixer_kernel, tt=tt, cast_next=cast_next),
        grid=(n_steps,),
        in_specs=in_specs,
        out_specs=[o[1] for o in outs],
        out_shape=[o[0] for o in outs],
        scratch_shapes=[
            pltpu.VMEM((tt, D_MODEL), BF16),
            pltpu.VMEM((tt, X0), F32),
            pltpu.VMEM((tt, LRU_WIDTH), F32),
            pltpu.VMEM((LRU_WIDTH // LANES, tt, LANES), F32),
            pltpu.VMEM((QK_COLS // LANES, tt, LANES), F32),
            pltpu.VMEM((2 * QK_COLS // LANES, tt, LANES), F32),
            pltpu.VMEM((tt, QK_COLS), F32),
            pltpu.VMEM((tt, QK_COLS), BF16),
            pltpu.VMEM((tt, QK_COLS), BF16),
            pltpu.VMEM((tt, QK_COLS), BF16),
            pltpu.VMEM((tt, D_MODEL), BF16),
            pltpu.VMEM((GLA_DV, QK_COLS), F32),
            pltpu.VMEM((LRU_CONV - 1, SUBLANES, LRU_WIDTH), F32),
            pltpu.VMEM((seg, SUBLANES, LRU_WIDTH), F32),
            pltpu.VMEM((seg, SUBLANES, LRU_WIDTH), F32),
            pltpu.VMEM((LRU_WIDTH // LANES, tt, LANES), F32),
            pltpu.VMEM((1, LRU_WIDTH), F32),
        ],
        compiler_params=pltpu.CompilerParams(
            dimension_semantics=("arbitrary",), vmem_limit_bytes=VMEM_LIMIT_BYTES),
        name="mixer",
    )(*args)
    return res[0], tuple(res[1:3]), tuple(res[3:])


def _ffn_call(h, layer, ffn_w, stacked, lnf, final_norm):
    t = h.shape[0]
    tt = _time_tile(t)
    win, wdown = ffn_w
    ln, cw, cb = stacked
    row_spec = pl.BlockSpec((tt, D_MODEL), lambda i: (i, 0))
    lspec = lambda a: _layer_spec(a, layer)
    return pl.pallas_call(
        functools.partial(_ffn_kernel, tt=tt, final_norm=final_norm),
        grid=(t // tt,),
        in_specs=[row_spec, lspec(ln), _const_spec(win.shape), lspec(cw), lspec(cb), _const_spec(wdown.shape),
                  _const_spec(lnf.shape)],
        out_specs=row_spec,
        out_shape=jax.ShapeDtypeStruct((t, D_MODEL), F32),
        scratch_shapes=[
            pltpu.VMEM((D_MODEL // LANES, tt, LANES), F32),
            pltpu.VMEM((D_MODEL // LANES, tt, LANES), F32),
            pltpu.VMEM((FFN_CONV - 1, SUBLANES, 2 * FFN_HIDDEN), F32),
            pltpu.VMEM((tt, FFN_HIDDEN), BF16),
        ],
        compiler_params=pltpu.CompilerParams(
            dimension_semantics=("arbitrary",), vmem_limit_bytes=VMEM_LIMIT_BYTES),
        name="ffn",
    )(h, ln, win, cw, cb, wdown, lnf)


def _block_diag_tiles(w):
    depth = w.shape[0]
    per = MXU_DIM // LRU_BLOCK
    w5 = w.reshape(depth, LRU_BLOCKS // per, per, LRU_BLOCK, LRU_BLOCK)
    tiles = jnp.einsum("lsjab,jk->lsjakb", w5, jnp.eye(per, dtype=w.dtype))
    return tiles.reshape(depth, LRU_BLOCKS // per, MXU_DIM, MXU_DIM).astype(BF16)


def kernel(x, ln_mix, w_in, gla_gate_w2, gla_gate_b, gla_norm, lru_conv_w, lru_conv_b,
           lru_wa, lru_ba, lru_wx, lru_bx, lru_lambda, w_out, ln_ffn, ffn_w_in,
           ffn_conv_w, ffn_conv_b, ffn_w_down, ln_final):
    bsz, t, d = x.shape
    assert bsz == 1 and d == D_MODEL and w_in.shape[2] == D_IN
    depth = w_in.shape[0]
    h = x.reshape(t, d)
    rows = lambda p: p.reshape(depth, 1, -1)
    w2 = jnp.pad(gla_gate_w2, ((0, 0), (0, GATE_PAD - GLA_GATE_RANK), (0, 0))).astype(BF16)
    mixer_small = (rows(ln_mix), w2, rows(gla_gate_b), rows(gla_norm), lru_conv_w, rows(lru_conv_b),
                   _block_diag_tiles(lru_wa), rows(lru_ba), _block_diag_tiles(lru_wx), rows(lru_bx),
                   rows(lru_lambda))
    ffn_small = (rows(ln_ffn), ffn_conv_w, rows(ffn_conv_b))
    raw = (ffn_w_in, ffn_w_down, w_in, w_out)
    w0 = w_in[0]
    mixer_w = (w0[:, :GATE0].astype(BF16), w0[:, GATE0 + GLA_GATE_RANK:].astype(BF16),
               jnp.pad(w0[:, GATE0:GATE0 + GLA_GATE_RANK], ((0, 0), (0, GATE_PAD - GLA_GATE_RANK))).astype(BF16),
               w_out[0].astype(BF16))
    for l in range(depth):
        h, ffn_w, next_mixer_w = _mixer_call(h, l, mixer_w, mixer_small, raw)
        h = _ffn_call(h, l, ffn_w, ffn_small, ln_final.reshape(1, -1), final_norm=(l == depth - 1))
        mixer_w = next_mixer_w
    return h.reshape(bsz, t, d)
```

```python
import functools
import math

import jax
import jax.numpy as jnp
from jax import lax
from jax.experimental import pallas as pl
from jax.experimental.pallas import tpu as pltpu

D_MODEL = 1024
GLA_HEADS = 4
GLA_DV = 128
GLA_DK = 64
GLA_GATE_RANK = 16
GLA_GATE_NORMALIZER = 16.0
GLA_CHUNK = 64
GLA_WIDTH = GLA_HEADS * GLA_DV
QK_COLS = GLA_HEADS * GLA_DK
LRU_WIDTH = D_MODEL - GLA_WIDTH
LRU_BLOCKS = 8
LRU_BLOCK = LRU_WIDTH // LRU_BLOCKS
LRU_CONV = 4
LRU_C = 8.0
FFN_HIDDEN = 3 * D_MODEL
FFN_CONV = 3
EPS = 1e-6

LANES = 128
SUBLANES = 8
MXU_DIM = 256
VMEM_LIMIT_BYTES = 56 * 1024 * 1024

Q0, K0, V0, G0, X0 = 0, 256, 512, 1024, 1536
GATE0 = X0
D_IN = GATE0 + GLA_GATE_RANK + 2 * LRU_WIDTH
QKVG_PIECES = ((0, 512), (512, 1024), (1024, 1280), (1280, 1536))
GATE_PAD = LANES
FFN_COL_BLOCK = 512
TIME_TILE = 1024

F32 = jnp.float32
BF16 = jnp.bfloat16


def _dot(a, b):
    return jnp.dot(a, b, preferred_element_type=F32)


def _dot_nt(a, b):
    return lax.dot_general(a, b, (((1,), (1,)), ((), ())), preferred_element_type=F32)


def _dot_tn(a, b):
    return lax.dot_general(a, b, (((0,), (0,)), ((), ())), preferred_element_type=F32)


def _rmsnorm(x, g):
    return x * lax.rsqrt(jnp.mean(x * x, axis=-1, keepdims=True) + EPS) * g


def _softplus(x):
    return jnp.maximum(x, 0.0) + jnp.log1p(jnp.exp(-jnp.abs(x)))


def _sigmoid(x):
    return 1.0 / (1.0 + jnp.exp(-x))


def _gelu_tanh(x):
    c = math.sqrt(2.0 / math.pi)
    hx = 0.5 * x
    return hx + hx * jnp.tanh(x * (c + (c * 0.044715) * (x * x)))


def _interleave_rows(src, slab_ref):
    n, tt, _ = slab_ref.shape
    seg = tt // SUBLANES
    for c in range(n):
        for s in range(SUBLANES):
            slab_ref[c, pl.ds(s, seg, stride=SUBLANES), :] = (
                src[s * seg:(s + 1) * seg, c * LANES:(c + 1) * LANES])


def _time_ordered_block(slab_ref, c, s):
    seg = slab_ref.shape[1] // SUBLANES
    return slab_ref[c, pl.ds(s, seg, stride=SUBLANES), :]


def _slabs_value(slab_ref, lo=0, hi=None):
    hi = slab_ref.shape[0] if hi is None else hi
    return jnp.concatenate([slab_ref[c] for c in range(lo, hi)], axis=1)


def _segment_history(tail, prev_tail):
    first = lax.broadcasted_iota(jnp.int32, tail.shape, 1) == 0
    return jnp.where(first, pltpu.roll(prev_tail, 1, axis=1), pltpu.roll(tail, 1, axis=1))


def _causal_conv_interleaved(x3, before, w_ref, b_ref, cols):
    taps = w_ref.shape[0]
    hist = taps - 1
    seg = x3.shape[0]
    out = b_ref[:, cols] + x3 * w_ref[hist:taps, cols]
    for k in range(1, taps):
        xk = jnp.concatenate([before[hist - k:], x3[:seg - k]], axis=0)
        out = out + xk * w_ref[hist - k:hist - k + 1, cols]
    return out


def _mixer_kernel(h_ref, ln_ref, wqkvg_ref, wxy_ref, wgate_ref, w2_ref, b2_ref, gnorm_ref,
                  cw_ref, cb_ref, wa_ref, ba_ref, wx_ref, bx_ref, lam_ref, wout_ref,
                  *refs, tt, cast_next):
    n_cast = 3 if cast_next else 2
    cast_in, out_ref, cast_out, scratch = refs[:n_cast], refs[n_cast], refs[n_cast + 1:2 * n_cast + 1], refs[2 * n_cast + 1:]
    _cast_weight_rows(cast_in, cast_out)
    _mixer_body(h_ref, ln_ref, wqkvg_ref, wxy_ref, wgate_ref, w2_ref, b2_ref, gnorm_ref,
                cw_ref, cb_ref, wa_ref, ba_ref, wx_ref, bx_ref, lam_ref, wout_ref, out_ref, *scratch, tt=tt)


def _cast_weight_rows(cast_in, cast_out):
    for src_ref, dst_ref in zip(cast_in, cast_out):
        dst_ref[...] = src_ref[...].astype(BF16)


def _mixer_body(h_ref, ln_ref, wqkvg_ref, wxy_ref, wgate_ref, w2_ref, b2_ref, gnorm_ref,
                cw_ref, cb_ref, wa_ref, ba_ref, wx_ref, bx_ref, lam_ref, wout_ref,
                out_ref,
                ub_ref, qkvg_ref, yg_ref, xil_ref, lail_ref, bil_ref, bnat_ref, qi_ref, ki_ref, kd_ref,
                y_ref, st_ref, a_ref, u_ref, sts_ref, xtail_ref, hs_ref, ps_ref, hil_ref, hlast_ref, *, tt):
    i = pl.program_id(0)
    seg = tt // SUBLANES

    @pl.when(i == 0)
    def _init():
        st_ref[...] = jnp.zeros_like(st_ref)
        xtail_ref[...] = jnp.zeros_like(xtail_ref)
        hlast_ref[...] = jnp.zeros_like(hlast_ref)

    ub_ref[...] = _rmsnorm(h_ref[...], ln_ref[...]).astype(BF16)

    _interleave_rows(_dot(ub_ref[...], wxy_ref[:, :LRU_WIDTH]), xil_ref)
    yg_ref[...] = _dot(ub_ref[...], wxy_ref[:, LRU_WIDTH:])
    x3 = _slabs_value(xil_ref).reshape(seg, SUBLANES, LRU_WIDTH)
    tail = x3[seg - (LRU_CONV - 1):]
    before = _segment_history(tail, xtail_ref[...])
    xtail_ref[...] = tail
    xc = _causal_conv_interleaved(x3, before, cw_ref, cb_ref, slice(None)).reshape(tt, LRU_WIDTH)
    neg_c_softplus = (-LRU_C) * _softplus(-lam_ref[...])
    half = LRU_WIDTH // 2
    jb = seg // len(QKVG_PIECES)
    rb = jb * SUBLANES
    hrun = prun = None
    for blk, (c_lo, c_hi) in enumerate(QKVG_PIECES):
        xcr = xc[blk * rb:(blk + 1) * rb]
        xcb = xcr.astype(BF16)
        ra = jnp.concatenate([_dot(xcb[:, :half], wa_ref[0]), _dot(xcb[:, half:], wa_ref[1])], axis=1)
        rx = jnp.concatenate([_dot(xcb[:, :half], wx_ref[0]), _dot(xcb[:, half:], wx_ref[1])], axis=1)
        r_gate = _sigmoid(ra + ba_ref[...])
        i_gate = _sigmoid(rx + bx_ref[...])
        log_a = r_gate * neg_c_softplus
        a = jnp.exp(log_a)
        th = jnp.tanh(log_a)
        mult = jnp.sqrt((-2.0 * th) / (1.0 - th))
        if blk == 0:
            row = lax.broadcasted_iota(jnp.int32, (rb, LRU_WIDTH), 0)
            mult = jnp.where(jnp.logical_and(row == 0, i == 0), 1.0, mult)
        u3 = (mult * (i_gate * xcr)).reshape(jb, SUBLANES, LRU_WIDTH)
        a3 = a.reshape(jb, SUBLANES, LRU_WIDTH)
        for j in range(jb):
            if hrun is None:
                hrun, prun = u3[0], a3[0]
            else:
                hrun = a3[j] * hrun + u3[j]
                prun = a3[j] * prun
            hs_ref[blk * jb + j] = hrun
            ps_ref[blk * jb + j] = prun
        qkvg_ref[:, c_lo:c_hi] = _dot(ub_ref[...], wqkvg_ref[:, c_lo:c_hi])
    sub = lax.broadcasted_iota(jnp.int32, (SUBLANES, LRU_WIDTH), 0)
    h_prev = hlast_ref[...]
    cu = jnp.where(sub == 0, hrun + prun * h_prev, hrun)
    ca = prun
    d = 1
    while d < SUBLANES:
        cu = ca * jnp.where(sub >= d, pltpu.roll(cu, d, axis=0), 0.0) + cu
        if 2 * d < SUBLANES:
            ca = ca * jnp.where(sub >= d, pltpu.roll(ca, d, axis=0), 1.0)
        d *= 2
    h_enter = jnp.where(sub == 0, h_prev, pltpu.roll(cu, 1, axis=0))
    hlast_ref[...] = cu[SUBLANES - 1:]
    h2 = (hs_ref[...] + ps_ref[...] * h_enter[None]).reshape(tt, LRU_WIDTH)
    for c in range(LRU_WIDTH // LANES):
        hil_ref[c] = h2[:, c * LANES:(c + 1) * LANES]
    glr = _dot(ub_ref[...], wgate_ref[...])
    for c in range(LRU_WIDTH // LANES):
        for s in range(SUBLANES):
            rows = slice(s * seg, (s + 1) * seg)
            lanes = slice(c * LANES, (c + 1) * LANES)
            y_ref[rows, GLA_WIDTH + c * LANES:GLA_WIDTH + (c + 1) * LANES] = (
                _time_ordered_block(hil_ref, c, s) * _gelu_tanh(yg_ref[rows, lanes])).astype(BF16)
    out_ref[...] = h_ref[...] + _dot(y_ref[:, GLA_WIDTH:], wout_ref[GLA_WIDTH:, :])

    gate = _dot(glr.astype(BF16), w2_ref[...]) + b2_ref[...]
    log_alpha = -_softplus(-gate) * (1.0 / GLA_GATE_NORMALIZER)
    _interleave_rows(log_alpha, lail_ref)
    la3 = _slabs_value(lail_ref).reshape(seg, SUBLANES, QK_COLS)
    nq = QK_COLS // LANES
    for c0 in range(0, seg, GLA_CHUNK):
        run = la3[c0]
        cum = [run]
        for j in range(c0 + 1, c0 + GLA_CHUNK):
            run = run + la3[j]
            cum.append(run)
        for j in range(GLA_CHUNK):
            for c in range(nq):
                lanes = slice(c * LANES, (c + 1) * LANES)
                rows = slice((c0 + j) * SUBLANES, (c0 + j + 1) * SUBLANES)
                bil_ref[c, rows, :] = cum[j][:, lanes]
                bil_ref[nq + c, rows, :] = (run - cum[j])[:, lanes]
    for s in range(SUBLANES):
        rows = slice(s * seg, (s + 1) * seg)
        b = jnp.concatenate([_time_ordered_block(bil_ref, c, s) for c in range(nq)], axis=1)
        bd = jnp.concatenate([_time_ordered_block(bil_ref, nq + c, s) for c in range(nq)], axis=1)
        q = qkvg_ref[rows, Q0:Q0 + QK_COLS]
        k = qkvg_ref[rows, K0:K0 + QK_COLS]
        bnat_ref[rows, :] = b
        qi_ref[rows, :] = ((q * (GLA_DK ** -0.5)) * jnp.exp(b)).astype(BF16)
        ki_ref[rows, :] = (k * jnp.exp(-b)).astype(BF16)
        kd_ref[rows, :] = (k * jnp.exp(bd)).astype(BF16)

    lane_head = lax.broadcasted_iota(jnp.int32, (GLA_CHUNK, QK_COLS), 1) // GLA_DK
    head_masks = [lane_head == hd for hd in range(GLA_HEADS)]
    c_row = lax.broadcasted_iota(jnp.int32, (GLA_HEADS * GLA_CHUNK, GLA_CHUNK), 0) % GLA_CHUNK
    c_col = lax.broadcasted_iota(jnp.int32, (GLA_HEADS * GLA_CHUNK, GLA_CHUNK), 1)
    causal = c_row >= c_col

    def stack_heads(x):
        zero = jnp.zeros_like(x)
        return jnp.concatenate([jnp.where(m, x, zero) for m in head_masks], axis=0)

    gnorm = gnorm_ref[...]
    n_chunks = tt // GLA_CHUNK
    chunk_rows = lambda c: slice(c * GLA_CHUNK, (c + 1) * GLA_CHUNK)
    head_cols = lambda hd: slice(hd * GLA_DV, (hd + 1) * GLA_DV)

    for c in range(n_chunks):
        rows = chunk_rows(c)
        qs = stack_heads(qi_ref[rows, :])
        ks = stack_heads(kd_ref[rows, :])
        v = qkvg_ref[rows, V0:V0 + GLA_WIDTH].astype(BF16)
        a_ref[c] = jnp.where(causal, _dot_nt(qs, ki_ref[rows, :]), 0.0).astype(BF16)
        v_stack = jnp.concatenate([v[:, head_cols(hd)] for hd in range(GLA_HEADS)], axis=0)
        u_ref[c] = _dot_tn(v_stack, ks)

    st = st_ref[...]
    for c in range(n_chunks):
        sts_ref[c] = st.astype(BF16)
        decay = jnp.exp(bnat_ref[(c + 1) * GLA_CHUNK - 1:(c + 1) * GLA_CHUNK, :])
        st = decay * st + u_ref[c]
    st_ref[...] = st

    out_group = n_chunks // 2
    for c in range(n_chunks):
        rows = chunk_rows(c)
        qs = stack_heads(qi_ref[rows, :])
        v = qkvg_ref[rows, V0:V0 + GLA_WIDTH].astype(BF16)
        o_inter = _dot_nt(qs, sts_ref[c])
        a_all = a_ref[c]
        for hd in range(GLA_HEADS):
            hr = slice(hd * GLA_CHUNK, (hd + 1) * GLA_CHUNK)
            vs = head_cols(hd)
            o = _dot(a_all[hr], v[:, vs]) + o_inter[hr]
            gh = qkvg_ref[rows, G0 + hd * GLA_DV:G0 + (hd + 1) * GLA_DV]
            y_ref[rows, vs] = (_rmsnorm(o, gnorm) * (gh * _sigmoid(gh))).astype(BF16)
        if (c + 1) % out_group == 0:
            orows = slice((c + 1 - out_group) * GLA_CHUNK, (c + 1) * GLA_CHUNK)
            out_ref[orows, :] = out_ref[orows, :] + _dot(y_ref[orows, :GLA_WIDTH], wout_ref[:GLA_WIDTH, :])


def _ffn_kernel(h_ref, ln_ref, win_ref, cw_ref, cb_ref, wdown_ref, lnf_ref, out_ref,
                hil_ref, oil_ref, tail_ref, act_ref, *, tt, final_norm):
    i = pl.program_id(0)
    seg = tt // SUBLANES
    hist = FFN_CONV - 1

    @pl.when(i == 0)
    def _init():
        tail_ref[...] = jnp.zeros_like(tail_ref)

    _interleave_rows(h_ref, hil_ref)
    h_in = _slabs_value(hil_ref)
    ub = _rmsnorm(h_in, ln_ref[...]).astype(BF16)

    def conv(c0):
        cols = slice(c0, c0 + FFN_COL_BLOCK)
        z = _dot(ub, win_ref[:, cols]).reshape(seg, SUBLANES, FFN_COL_BLOCK)
        tail = z[seg - hist:]
        before = _segment_history(tail, tail_ref[:, :, cols])
        tail_ref[:, :, cols] = tail
        return _causal_conv_interleaved(z, before, cw_ref, cb_ref, cols).reshape(tt, FFN_COL_BLOCK)

    for blk in range(FFN_HIDDEN // FFN_COL_BLOCK):
        c0 = blk * FFN_COL_BLOCK
        za = conv(c0)
        zg = conv(FFN_HIDDEN + c0)
        act_ref[:, c0:c0 + FFN_COL_BLOCK] = (_gelu_tanh(za) * zg).astype(BF16)

    out = h_in + _dot(act_ref[...], wdown_ref[...])
    if final_norm:
        out = _rmsnorm(out, lnf_ref[...])
    n_slabs = D_MODEL // LANES
    for c in range(n_slabs):
        oil_ref[c] = out[:, c * LANES:(c + 1) * LANES]
    for c in range(n_slabs):
        for s in range(SUBLANES):
            out_ref[s * seg:(s + 1) * seg, c * LANES:(c + 1) * LANES] = _time_ordered_block(oil_ref, c, s)


def _const_spec(shape):
    zeros = (0,) * len(shape)
    return pl.BlockSpec(shape, lambda i: zeros, pipeline_mode=pl.Buffered(1))


def _layer_spec(arr, layer):
    zeros = (0,) * (arr.ndim - 1)
    return pl.BlockSpec((None,) + arr.shape[1:], lambda i: (layer,) + zeros, pipeline_mode=pl.Buffered(1))


def _layer_rows_spec(arr, layer, n_steps):
    rows = arr.shape[1] // n_steps
    assert rows * n_steps == arr.shape[1] and rows % 16 == 0
    return pl.BlockSpec((None, rows, arr.shape[2]), lambda i: (layer, i, 0))


def _rows_out(n_rows, n_cols, n_steps):
    rows = n_rows // n_steps
    return (jax.ShapeDtypeStruct((n_rows, n_cols), BF16), pl.BlockSpec((rows, n_cols), lambda i: (i, 0)))


def _time_tile(t):
    assert t % TIME_TILE == 0 and (TIME_TILE // SUBLANES) % GLA_CHUNK == 0
    return TIME_TILE


def _mixer_call(h, layer, wout, stacked, raw):
    t = h.shape[0]
    tt = _time_tile(t)
    seg = tt // SUBLANES
    n_steps = t // tt
    depth = raw[0].shape[0]
    cast_next = layer + 1 < depth
    ffn_w_in, ffn_w_down, w_out = raw
    row_spec = pl.BlockSpec((tt, D_MODEL), lambda i: (i, 0))
    in_specs = ([row_spec] + [_layer_spec(a, layer) for a in stacked] + [_const_spec(wout.shape)]
                + [_layer_rows_spec(ffn_w_in, layer, n_steps), _layer_rows_spec(ffn_w_down, layer, n_steps)])
    args = [h, *stacked, wout, ffn_w_in, ffn_w_down]
    outs = [(jax.ShapeDtypeStruct((t, D_MODEL), F32), row_spec),
            _rows_out(D_MODEL, 2 * FFN_HIDDEN, n_steps), _rows_out(FFN_HIDDEN, D_MODEL, n_steps)]
    if cast_next:
        in_specs.append(_layer_rows_spec(w_out, layer + 1, n_steps))
        args.append(w_out)
        outs.append(_rows_out(D_MODEL, D_MODEL, n_steps))
    res = pl.pallas_call(
        functools.partial(_mixer_kernel, tt=tt, cast_next=cast_next),
        grid=(n_steps,),
        in_specs=in_specs,
        out_specs=[o[1] for o in outs],
        out_shape=[o[0] for o in outs],
        scratch_shapes=[
            pltpu.VMEM((tt, D_MODEL), BF16),
            pltpu.VMEM((tt, X0), F32),
            pltpu.VMEM((tt, LRU_WIDTH), F32),
            pltpu.VMEM((LRU_WIDTH // LANES, tt, LANES), F32),
            pltpu.VMEM((QK_COLS // LANES, tt, LANES), F32),
            pltpu.VMEM((2 * QK_COLS // LANES, tt, LANES), F32),
            pltpu.VMEM((tt, QK_COLS), F32),
            pltpu.VMEM((tt, QK_COLS), BF16),
            pltpu.VMEM((tt, QK_COLS), BF16),
            pltpu.VMEM((tt, QK_COLS), BF16),
            pltpu.VMEM((tt, D_MODEL), BF16),
            pltpu.VMEM((GLA_DV, QK_COLS), F32),
            pltpu.VMEM((tt // GLA_CHUNK, GLA_HEADS * GLA_CHUNK, GLA_CHUNK), BF16),
            pltpu.VMEM((tt // GLA_CHUNK, GLA_DV, QK_COLS), F32),
            pltpu.VMEM((tt // GLA_CHUNK, GLA_DV, QK_COLS), BF16),
            pltpu.VMEM((LRU_CONV - 1, SUBLANES, LRU_WIDTH), F32),
            pltpu.VMEM((seg, SUBLANES, LRU_WIDTH), F32),
            pltpu.VMEM((seg, SUBLANES, LRU_WIDTH), F32),
            pltpu.VMEM((LRU_WIDTH // LANES, tt, LANES), F32),
            pltpu.VMEM((1, LRU_WIDTH), F32),
        ],
        compiler_params=pltpu.CompilerParams(
            dimension_semantics=("arbitrary",), vmem_limit_bytes=VMEM_LIMIT_BYTES),
        name="mixer",
    )(*args)
    return res[0], tuple(res[1:3]), (res[3] if cast_next else None)


def _ffn_call(h, layer, ffn_w, stacked, lnf, final_norm):
    t = h.shape[0]
    tt = _time_tile(t)
    win, wdown = ffn_w
    ln, cw, cb = stacked
    row_spec = pl.BlockSpec((tt, D_MODEL), lambda i: (i, 0))
    lspec = lambda a: _layer_spec(a, layer)
    return pl.pallas_call(
        functools.partial(_ffn_kernel, tt=tt, final_norm=final_norm),
        grid=(t // tt,),
        in_specs=[row_spec, lspec(ln), _const_spec(win.shape), lspec(cw), lspec(cb), _const_spec(wdown.shape),
                  _const_spec(lnf.shape)],
        out_specs=row_spec,
        out_shape=jax.ShapeDtypeStruct((t, D_MODEL), F32),
        scratch_shapes=[
            pltpu.VMEM((D_MODEL // LANES, tt, LANES), F32),
            pltpu.VMEM((D_MODEL // LANES, tt, LANES), F32),
            pltpu.VMEM((FFN_CONV - 1, SUBLANES, 2 * FFN_HIDDEN), F32),
            pltpu.VMEM((tt, FFN_HIDDEN), BF16),
        ],
        compiler_params=pltpu.CompilerParams(
            dimension_semantics=("arbitrary",), vmem_limit_bytes=VMEM_LIMIT_BYTES),
        name="ffn",
    )(h, ln, win, cw, cb, wdown, lnf)


def _block_diag_tiles(w):
    depth = w.shape[0]
    per = MXU_DIM // LRU_BLOCK
    w5 = w.reshape(depth, LRU_BLOCKS // per, per, LRU_BLOCK, LRU_BLOCK)
    tiles = jnp.einsum("lsjab,jk->lsjakb", w5, jnp.eye(per, dtype=w.dtype))
    return tiles.reshape(depth, LRU_BLOCKS // per, MXU_DIM, MXU_DIM).astype(BF16)


def kernel(x, ln_mix, w_in, gla_gate_w2, gla_gate_b, gla_norm, lru_conv_w, lru_conv_b,
           lru_wa, lru_ba, lru_wx, lru_bx, lru_lambda, w_out, ln_ffn, ffn_w_in,
           ffn_conv_w, ffn_conv_b, ffn_w_down, ln_final):
    bsz, t, d = x.shape
    assert bsz == 1 and d == D_MODEL and w_in.shape[2] == D_IN
    depth = w_in.shape[0]
    h = x.reshape(t, d)
    rows = lambda p: p.reshape(depth, 1, -1)
    gate1 = GATE0 + GLA_GATE_RANK
    w2 = jnp.pad(gla_gate_w2, ((0, 0), (0, GATE_PAD - GLA_GATE_RANK), (0, 0))).astype(BF16)
    wgate = jnp.pad(w_in[:, :, GATE0:gate1], ((0, 0), (0, 0), (0, GATE_PAD - GLA_GATE_RANK))).astype(BF16)
    mixer_stacked = (rows(ln_mix), w_in[:, :, :GATE0].astype(BF16), w_in[:, :, gate1:].astype(BF16), wgate,
                     w2, rows(gla_gate_b), rows(gla_norm), lru_conv_w, rows(lru_conv_b),
                     _block_diag_tiles(lru_wa), rows(lru_ba), _block_diag_tiles(lru_wx), rows(lru_bx),
                     rows(lru_lambda))
    ffn_small = (rows(ln_ffn), ffn_conv_w, rows(ffn_conv_b))
    raw = (ffn_w_in, ffn_w_down, w_out)
    wout = w_out[0].astype(BF16)
    for l in range(depth):
        h, ffn_w, wout = _mixer_call(h, l, wout, mixer_stacked, raw)
        h = _ffn_call(h, l, ffn_w, ffn_small, ln_final.reshape(1, -1), final_norm=(l == depth - 1))
    return h.reshape(bsz, t, d)
```

```python
import functools
import math

import jax
import jax.numpy as jnp
from jax import lax
from jax.experimental import pallas as pl
from jax.experimental.pallas import tpu as pltpu

D_MODEL = 1024
GLA_HEADS = 4
GLA_DV = 128
GLA_DK = 64
GLA_GATE_RANK = 16
GLA_GATE_NORMALIZER = 16.0
GLA_CHUNK = 64
GLA_WIDTH = GLA_HEADS * GLA_DV
QK_COLS = GLA_HEADS * GLA_DK
LRU_WIDTH = D_MODEL - GLA_WIDTH
LRU_BLOCKS = 8
LRU_BLOCK = LRU_WIDTH // LRU_BLOCKS
LRU_CONV = 4
LRU_C = 8.0
FFN_HIDDEN = 3 * D_MODEL
FFN_CONV = 3
EPS = 1e-6

LANES = 128
SUBLANES = 8
MXU_DIM = 256
VMEM_LIMIT_BYTES = 56 * 1024 * 1024
MIXER_VMEM_LIMIT_BYTES = 53 * 1024 * 1024

Q0, K0, V0, G0, X0 = 0, 256, 512, 1024, 1536
GATE0 = X0
D_IN = GATE0 + GLA_GATE_RANK + 2 * LRU_WIDTH
QKVG_PIECES = ((0, 512), (512, 1024), (1024, 1280), (1280, 1536))
GATE_PAD = LANES
FFN_COL_BLOCK = 512
TIME_TILE = 1024

F32 = jnp.float32
BF16 = jnp.bfloat16


def _dot(a, b):
    return jnp.dot(a, b, preferred_element_type=F32)


def _dot_nt(a, b):
    return lax.dot_general(a, b, (((1,), (1,)), ((), ())), preferred_element_type=F32)


def _dot_tn(a, b):
    return lax.dot_general(a, b, (((0,), (0,)), ((), ())), preferred_element_type=F32)


def _rmsnorm(x, g):
    return x * lax.rsqrt(jnp.mean(x * x, axis=-1, keepdims=True) + EPS) * g


def _softplus(x):
    return jnp.maximum(x, 0.0) + jnp.log1p(jnp.exp(-jnp.abs(x)))


def _sigmoid(x):
    return 1.0 / (1.0 + jnp.exp(-x))


def _gelu_tanh(x):
    c = math.sqrt(2.0 / math.pi)
    hx = 0.5 * x
    return hx + hx * jnp.tanh(x * (c + (c * 0.044715) * (x * x)))


def _interleave_rows(src, slab_ref):
    n, tt, _ = slab_ref.shape
    seg = tt // SUBLANES
    for c in range(n):
        for s in range(SUBLANES):
            slab_ref[c, pl.ds(s, seg, stride=SUBLANES), :] = (
                src[s * seg:(s + 1) * seg, c * LANES:(c + 1) * LANES])


def _time_ordered_block(slab_ref, c, s):
    seg = slab_ref.shape[1] // SUBLANES
    return slab_ref[c, pl.ds(s, seg, stride=SUBLANES), :]


def _slabs_value(slab_ref, lo=0, hi=None):
    hi = slab_ref.shape[0] if hi is None else hi
    return jnp.concatenate([slab_ref[c] for c in range(lo, hi)], axis=1)


def _segment_history(tail, prev_tail):
    first = lax.broadcasted_iota(jnp.int32, tail.shape, 1) == 0
    return jnp.where(first, pltpu.roll(prev_tail, 1, axis=1), pltpu.roll(tail, 1, axis=1))


def _causal_conv_interleaved(x3, before, w_ref, b_ref, cols):
    taps = w_ref.shape[0]
    hist = taps - 1
    seg = x3.shape[0]
    out = b_ref[:, cols] + x3 * w_ref[hist:taps, cols]
    for k in range(1, taps):
        xk = jnp.concatenate([before[hist - k:], x3[:seg - k]], axis=0)
        out = out + xk * w_ref[hist - k:hist - k + 1, cols]
    return out


def _mixer_kernel(h_ref, ln_ref, win_ref, w2_ref, b2_ref, gnorm_ref,
                  cw_ref, cb_ref, wa_ref, ba_ref, wx_ref, bx_ref, lam_ref, wout_ref,
                  *refs, tt, cast_next):
    n_cast = 3 if cast_next else 2
    cast_in, out_ref, cast_out, scratch = refs[:n_cast], refs[n_cast], refs[n_cast + 1:2 * n_cast + 1], refs[2 * n_cast + 1:]
    _cast_weight_rows(cast_in, cast_out)
    _mixer_body(h_ref, ln_ref, win_ref, w2_ref, b2_ref, gnorm_ref,
                cw_ref, cb_ref, wa_ref, ba_ref, wx_ref, bx_ref, lam_ref, wout_ref, out_ref, *scratch, tt=tt)


def _cast_weight_rows(cast_in, cast_out):
    for src_ref, dst_ref in zip(cast_in, cast_out):
        dst_ref[...] = src_ref[...].astype(BF16)


def _mixer_body(h_ref, ln_ref, win_ref, w2_ref, b2_ref, gnorm_ref,
                cw_ref, cb_ref, wa_ref, ba_ref, wx_ref, bx_ref, lam_ref, wout_ref,
                out_ref,
                wxy_ref, wgate_ref, ub_ref, qkvg_ref, yg_ref, xil_ref, bnat_ref, qi_ref, ki_ref, kd_ref,
                y_ref, st_ref, a_ref, u_ref, sts_ref, xtail_ref, hs_ref, ps_ref, hil_ref, hlast_ref, *, tt):
    i = pl.program_id(0)
    seg = tt // SUBLANES

    @pl.when(i == 0)
    def _init():
        st_ref[...] = jnp.zeros_like(st_ref)
        xtail_ref[...] = jnp.zeros_like(xtail_ref)
        hlast_ref[...] = jnp.zeros_like(hlast_ref)
        wxy_ref[...] = win_ref[:, GATE0 + GLA_GATE_RANK:]
        lane = lax.broadcasted_iota(jnp.int32, wgate_ref.shape, 1)
        wgate_ref[...] = jnp.where(lane < GLA_GATE_RANK, win_ref[:, GATE0:GATE0 + GATE_PAD], jnp.zeros_like(wgate_ref))

    ub_ref[...] = _rmsnorm(h_ref[...], ln_ref[...]).astype(BF16)

    _interleave_rows(_dot(ub_ref[...], wxy_ref[:, :LRU_WIDTH]), xil_ref)
    yg_ref[...] = _dot(ub_ref[...], wxy_ref[:, LRU_WIDTH:])
    x3 = _slabs_value(xil_ref).reshape(seg, SUBLANES, LRU_WIDTH)
    tail = x3[seg - (LRU_CONV - 1):]
    before = _segment_history(tail, xtail_ref[...])
    xtail_ref[...] = tail
    xc = _causal_conv_interleaved(x3, before, cw_ref, cb_ref, slice(None)).reshape(tt, LRU_WIDTH)
    neg_c_softplus = (-LRU_C) * _softplus(-lam_ref[...])
    half = LRU_WIDTH // 2
    jb = seg // len(QKVG_PIECES)
    rb = jb * SUBLANES
    hrun = prun = None
    for blk, (c_lo, c_hi) in enumerate(QKVG_PIECES):
        xcr = xc[blk * rb:(blk + 1) * rb]
        xcb = xcr.astype(BF16)
        ra = jnp.concatenate([_dot(xcb[:, :half], wa_ref[0]), _dot(xcb[:, half:], wa_ref[1])], axis=1)
        rx = jnp.concatenate([_dot(xcb[:, :half], wx_ref[0]), _dot(xcb[:, half:], wx_ref[1])], axis=1)
        r_gate = _sigmoid(ra + ba_ref[...])
        i_gate = _sigmoid(rx + bx_ref[...])
        log_a = r_gate * neg_c_softplus
        a = jnp.exp(log_a)
        th = jnp.tanh(log_a)
        mult = jnp.sqrt((-2.0 * th) / (1.0 - th))
        if blk == 0:
            row = lax.broadcasted_iota(jnp.int32, (rb, LRU_WIDTH), 0)
            mult = jnp.where(jnp.logical_and(row == 0, i == 0), 1.0, mult)
        u3 = (mult * (i_gate * xcr)).reshape(jb, SUBLANES, LRU_WIDTH)
        a3 = a.reshape(jb, SUBLANES, LRU_WIDTH)
        for j in range(jb):
            if hrun is None:
                hrun, prun = u3[0], a3[0]
            else:
                hrun = a3[j] * hrun + u3[j]
                prun = a3[j] * prun
            hs_ref[blk * jb + j] = hrun
            ps_ref[blk * jb + j] = prun
        qkvg_ref[:, c_lo:c_hi] = _dot(ub_ref[...], win_ref[:, c_lo:c_hi])
    sub = lax.broadcasted_iota(jnp.int32, (SUBLANES, LRU_WIDTH), 0)
    h_prev = hlast_ref[...]
    cu = jnp.where(sub == 0, hrun + prun * h_prev, hrun)
    ca = prun
    d = 1
    while d < SUBLANES:
        cu = ca * jnp.where(sub >= d, pltpu.roll(cu, d, axis=0), 0.0) + cu
        if 2 * d < SUBLANES:
            ca = ca * jnp.where(sub >= d, pltpu.roll(ca, d, axis=0), 1.0)
        d *= 2
    h_enter = jnp.where(sub == 0, h_prev, pltpu.roll(cu, 1, axis=0))
    hlast_ref[...] = cu[SUBLANES - 1:]
    h2 = (hs_ref[...] + ps_ref[...] * h_enter[None]).reshape(tt, LRU_WIDTH)
    for c in range(LRU_WIDTH // LANES):
        hil_ref[c] = h2[:, c * LANES:(c + 1) * LANES]
    glr = _dot(ub_ref[...], wgate_ref[...])
    for c in range(LRU_WIDTH // LANES):
        for s in range(SUBLANES):
            rows = slice(s * seg, (s + 1) * seg)
            lanes = slice(c * LANES, (c + 1) * LANES)
            y_ref[rows, GLA_WIDTH + c * LANES:GLA_WIDTH + (c + 1) * LANES] = (
                _time_ordered_block(hil_ref, c, s) * _gelu_tanh(yg_ref[rows, lanes])).astype(BF16)
    out_ref[...] = h_ref[...] + _dot(y_ref[:, GLA_WIDTH:], wout_ref[GLA_WIDTH:, :])

    gate = _dot(glr.astype(BF16), w2_ref[...]) + b2_ref[...]
    log_alpha = -_softplus(-gate) * (1.0 / GLA_GATE_NORMALIZER)
    lail_ref = hil_ref.at[0:QK_COLS // LANES]
    bil_ref = xil_ref
    _interleave_rows(log_alpha, lail_ref)
    la3 = _slabs_value(lail_ref).reshape(seg, SUBLANES, QK_COLS)
    nq = QK_COLS // LANES
    for c0 in range(0, seg, GLA_CHUNK):
        run = la3[c0]
        cum = [run]
        for j in range(c0 + 1, c0 + GLA_CHUNK):
            run = run + la3[j]
            cum.append(run)
        for j in range(GLA_CHUNK):
            for c in range(nq):
                lanes = slice(c * LANES, (c + 1) * LANES)
                rows = slice((c0 + j) * SUBLANES, (c0 + j + 1) * SUBLANES)
                bil_ref[c, rows, :] = cum[j][:, lanes]
                bil_ref[nq + c, rows, :] = (run - cum[j])[:, lanes]
    for s in range(SUBLANES):
        rows = slice(s * seg, (s + 1) * seg)
        b = jnp.concatenate([_time_ordered_block(bil_ref, c, s) for c in range(nq)], axis=1)
        bd = jnp.concatenate([_time_ordered_block(bil_ref, nq + c, s) for c in range(nq)], axis=1)
        q = qkvg_ref[rows, Q0:Q0 + QK_COLS]
        k = qkvg_ref[rows, K0:K0 + QK_COLS]
        bnat_ref[rows, :] = b
        qi_ref[rows, :] = ((q * (GLA_DK ** -0.5)) * jnp.exp(b)).astype(BF16)
        ki_ref[rows, :] = (k * jnp.exp(-b)).astype(BF16)
        kd_ref[rows, :] = (k * jnp.exp(bd)).astype(BF16)

    lane_head = lax.broadcasted_iota(jnp.int32, (GLA_CHUNK, QK_COLS), 1) // GLA_DK
    head_masks = [lane_head == hd for hd in range(GLA_HEADS)]
    c_row = lax.broadcasted_iota(jnp.int32, (GLA_HEADS * GLA_CHUNK, GLA_CHUNK), 0) % GLA_CHUNK
    c_col = lax.broadcasted_iota(jnp.int32, (GLA_HEADS * GLA_CHUNK, GLA_CHUNK), 1)
    causal = c_row >= c_col

    def stack_heads(x):
        zero = jnp.zeros_like(x)
        return jnp.concatenate([jnp.where(m, x, zero) for m in head_masks], axis=0)

    gnorm = gnorm_ref[...]
    n_chunks = tt // GLA_CHUNK
    chunk_rows = lambda c: slice(c * GLA_CHUNK, (c + 1) * GLA_CHUNK)
    head_cols = lambda hd: slice(hd * GLA_DV, (hd + 1) * GLA_DV)

    for c in range(n_chunks):
        rows = chunk_rows(c)
        qs = stack_heads(qi_ref[rows, :])
        ks = stack_heads(kd_ref[rows, :])
        v = qkvg_ref[rows, V0:V0 + GLA_WIDTH].astype(BF16)
        a_ref[c] = jnp.where(causal, _dot_nt(qs, ki_ref[rows, :]), 0.0).astype(BF16)
        v_stack = jnp.concatenate([v[:, head_cols(hd)] for hd in range(GLA_HEADS)], axis=0)
        u_ref[c] = _dot_tn(v_stack, ks)

    st = st_ref[...]
    for c in range(n_chunks):
        sts_ref[c] = st.astype(BF16)
        decay = jnp.exp(bnat_ref[(c + 1) * GLA_CHUNK - 1:(c + 1) * GLA_CHUNK, :])
        st = decay * st + u_ref[c]
    st_ref[...] = st

    out_group = n_chunks // 2
    for c in range(n_chunks):
        rows = chunk_rows(c)
        qs = stack_heads(qi_ref[rows, :])
        v = qkvg_ref[rows, V0:V0 + GLA_WIDTH].astype(BF16)
        o_inter = _dot_nt(qs, sts_ref[c])
        a_all = a_ref[c]
        for hd in range(GLA_HEADS):
            hr = slice(hd * GLA_CHUNK, (hd + 1) * GLA_CHUNK)
            vs = head_cols(hd)
            o = _dot(a_all[hr], v[:, vs]) + o_inter[hr]
            gh = qkvg_ref[rows, G0 + hd * GLA_DV:G0 + (hd + 1) * GLA_DV]
            y_ref[rows, vs] = (_rmsnorm(o, gnorm) * (gh * _sigmoid(gh))).astype(BF16)
        if (c + 1) % out_group == 0:
            orows = slice((c + 1 - out_group) * GLA_CHUNK, (c + 1) * GLA_CHUNK)
            out_ref[orows, :] = out_ref[orows, :] + _dot(y_ref[orows, :GLA_WIDTH], wout_ref[:GLA_WIDTH, :])


def _ffn_kernel(h_ref, ln_ref, win_ref, cw_ref, cb_ref, wdown_ref, lnf_ref, out_ref,
                hil_ref, oil_ref, tail_ref, act_ref, *, tt, final_norm):
    i = pl.program_id(0)
    seg = tt // SUBLANES
    hist = FFN_CONV - 1

    @pl.when(i == 0)
    def _init():
        tail_ref[...] = jnp.zeros_like(tail_ref)

    _interleave_rows(h_ref, hil_ref)
    h_in = _slabs_value(hil_ref)
    ub = _rmsnorm(h_in, ln_ref[...]).astype(BF16)

    def conv(c0):
        cols = slice(c0, c0 + FFN_COL_BLOCK)
        z = _dot(ub, win_ref[:, cols]).reshape(seg, SUBLANES, FFN_COL_BLOCK)
        tail = z[seg - hist:]
        before = _segment_history(tail, tail_ref[:, :, cols])
        tail_ref[:, :, cols] = tail
        return _causal_conv_interleaved(z, before, cw_ref, cb_ref, cols).reshape(tt, FFN_COL_BLOCK)

    for blk in range(FFN_HIDDEN // FFN_COL_BLOCK):
        c0 = blk * FFN_COL_BLOCK
        za = conv(c0)
        zg = conv(FFN_HIDDEN + c0)
        act_ref[:, c0:c0 + FFN_COL_BLOCK] = (_gelu_tanh(za) * zg).astype(BF16)

    out = h_in + _dot(act_ref[...], wdown_ref[...])
    if final_norm:
        out = _rmsnorm(out, lnf_ref[...])
    n_slabs = D_MODEL // LANES
    for c in range(n_slabs):
        oil_ref[c] = out[:, c * LANES:(c + 1) * LANES]
    for c in range(n_slabs):
        for s in range(SUBLANES):
            out_ref[s * seg:(s + 1) * seg, c * LANES:(c + 1) * LANES] = _time_ordered_block(oil_ref, c, s)


def _const_spec(shape):
    zeros = (0,) * len(shape)
    return pl.BlockSpec(shape, lambda i: zeros, pipeline_mode=pl.Buffered(1))


def _layer_spec(arr, layer):
    zeros = (0,) * (arr.ndim - 1)
    return pl.BlockSpec((None,) + arr.shape[1:], lambda i: (layer,) + zeros, pipeline_mode=pl.Buffered(1))


def _layer_rows_spec(arr, layer, n_steps):
    rows = arr.shape[1] // n_steps
    assert rows * n_steps == arr.shape[1] and rows % 16 == 0
    return pl.BlockSpec((None, rows, arr.shape[2]), lambda i: (layer, i, 0))


def _rows_out(n_rows, n_cols, n_steps):
    rows = n_rows // n_steps
    return (jax.ShapeDtypeStruct((n_rows, n_cols), BF16), pl.BlockSpec((rows, n_cols), lambda i: (i, 0)))


def _time_tile(t):
    assert t % TIME_TILE == 0 and (TIME_TILE // SUBLANES) % GLA_CHUNK == 0
    return TIME_TILE


def _mixer_call(h, layer, wout, stacked, raw):
    t = h.shape[0]
    tt = _time_tile(t)
    seg = tt // SUBLANES
    n_steps = t // tt
    depth = raw[0].shape[0]
    cast_next = layer + 1 < depth
    ffn_w_in, ffn_w_down, w_out = raw
    row_spec = pl.BlockSpec((tt, D_MODEL), lambda i: (i, 0))
    in_specs = ([row_spec] + [_layer_spec(a, layer) for a in stacked] + [_const_spec(wout.shape)]
                + [_layer_rows_spec(ffn_w_in, layer, n_steps), _layer_rows_spec(ffn_w_down, layer, n_steps)])
    args = [h, *stacked, wout, ffn_w_in, ffn_w_down]
    outs = [(jax.ShapeDtypeStruct((t, D_MODEL), F32), row_spec),
            _rows_out(D_MODEL, 2 * FFN_HIDDEN, n_steps), _rows_out(FFN_HIDDEN, D_MODEL, n_steps)]
    if cast_next:
        in_specs.append(_layer_rows_spec(w_out, layer + 1, n_steps))
        args.append(w_out)
        outs.append(_rows_out(D_MODEL, D_MODEL, n_steps))
    res = pl.pallas_call(
        functools.partial(_mixer_kernel, tt=tt, cast_next=cast_next),
        grid=(n_steps,),
        in_specs=in_specs,
        out_specs=[o[1] for o in outs],
        out_shape=[o[0] for o in outs],
        scratch_shapes=[
            pltpu.VMEM((D_MODEL, 2 * LRU_WIDTH), BF16),
            pltpu.VMEM((D_MODEL, GATE_PAD), BF16),
            pltpu.VMEM((tt, D_MODEL), BF16),
            pltpu.VMEM((tt, X0), F32),
            pltpu.VMEM((tt, LRU_WIDTH), F32),
            pltpu.VMEM((LRU_WIDTH // LANES, tt, LANES), F32),
            pltpu.VMEM((tt, QK_COLS), F32),
            pltpu.VMEM((tt, QK_COLS), BF16),
            pltpu.VMEM((tt, QK_COLS), BF16),
            pltpu.VMEM((tt, QK_COLS), BF16),
            pltpu.VMEM((tt, D_MODEL), BF16),
            pltpu.VMEM((GLA_DV, QK_COLS), F32),
            pltpu.VMEM((tt // GLA_CHUNK, GLA_HEADS * GLA_CHUNK, GLA_CHUNK), BF16),
            pltpu.VMEM((tt // GLA_CHUNK, GLA_DV, QK_COLS), F32),
            pltpu.VMEM((tt // GLA_CHUNK, GLA_DV, QK_COLS), BF16),
            pltpu.VMEM((LRU_CONV - 1, SUBLANES, LRU_WIDTH), F32),
            pltpu.VMEM((seg, SUBLANES, LRU_WIDTH), F32),
            pltpu.VMEM((seg, SUBLANES, LRU_WIDTH), F32),
            pltpu.VMEM((LRU_WIDTH // LANES, tt, LANES), F32),
            pltpu.VMEM((1, LRU_WIDTH), F32),
        ],
        compiler_params=pltpu.CompilerParams(
            dimension_semantics=("arbitrary",), vmem_limit_bytes=MIXER_VMEM_LIMIT_BYTES),
        name="mixer",
    )(*args)
    return res[0], tuple(res[1:3]), (res[3] if cast_next else None)


def _ffn_call(h, layer, ffn_w, stacked, lnf, final_norm):
    t = h.shape[0]
    tt = _time_tile(t)
    win, wdown = ffn_w
    ln, cw, cb = stacked
    row_spec = pl.BlockSpec((tt, D_MODEL), lambda i: (i, 0))
    lspec = lambda a: _layer_spec(a, layer)
    return pl.pallas_call(
        functools.partial(_ffn_kernel, tt=tt, final_norm=final_norm),
        grid=(t // tt,),
        in_specs=[row_spec, lspec(ln), _const_spec(win.shape), lspec(cw), lspec(cb), _const_spec(wdown.shape),
                  _const_spec(lnf.shape)],
        out_specs=row_spec,
        out_shape=jax.ShapeDtypeStruct((t, D_MODEL), F32),
        scratch_shapes=[
            pltpu.VMEM((D_MODEL // LANES, tt, LANES), F32),
            pltpu.VMEM((D_MODEL // LANES, tt, LANES), F32),
            pltpu.VMEM((FFN_CONV - 1, SUBLANES, 2 * FFN_HIDDEN), F32),
            pltpu.VMEM((tt, FFN_HIDDEN), BF16),
        ],
        compiler_params=pltpu.CompilerParams(
            dimension_semantics=("arbitrary",), vmem_limit_bytes=VMEM_LIMIT_BYTES),
        name="ffn",
    )(h, ln, win, cw, cb, wdown, lnf)


def _block_diag_tiles(w):
    depth = w.shape[0]
    per = MXU_DIM // LRU_BLOCK
    w5 = w.reshape(depth, LRU_BLOCKS // per, per, LRU_BLOCK, LRU_BLOCK)
    tiles = jnp.einsum("lsjab,jk->lsjakb", w5, jnp.eye(per, dtype=w.dtype))
    return tiles.reshape(depth, LRU_BLOCKS // per, MXU_DIM, MXU_DIM).astype(BF16)


def kernel(x, ln_mix, w_in, gla_gate_w2, gla_gate_b, gla_norm, lru_conv_w, lru_conv_b,
           lru_wa, lru_ba, lru_wx, lru_bx, lru_lambda, w_out, ln_ffn, ffn_w_in,
           ffn_conv_w, ffn_conv_b, ffn_w_down, ln_final):
    bsz, t, d = x.shape
    assert bsz == 1 and d == D_MODEL and w_in.shape[2] == D_IN
    depth = w_in.shape[0]
    h = x.reshape(t, d)
    rows = lambda p: p.reshape(depth, 1, -1)
    w2 = jnp.pad(gla_gate_w2, ((0, 0), (0, GATE_PAD - GLA_GATE_RANK), (0, 0))).astype(BF16)
    mixer_stacked = (rows(ln_mix), w_in.astype(BF16), w2, rows(gla_gate_b), rows(gla_norm), lru_conv_w,
                     rows(lru_conv_b), _block_diag_tiles(lru_wa), rows(lru_ba), _block_diag_tiles(lru_wx),
                     rows(lru_bx), rows(lru_lambda))
    ffn_small = (rows(ln_ffn), ffn_conv_w, rows(ffn_conv_b))
    raw = (ffn_w_in, ffn_w_down, w_out)
    wout = w_out[0].astype(BF16)
    for l in range(depth):
        h, ffn_w, wout = _mixer_call(h, l, wout, mixer_stacked, raw)
        h = _ffn_call(h, l, ffn_w, ffn_small, ln_final.reshape(1, -1), final_norm=(l == depth - 1))
    return h.reshape(bsz, t, d)
```

```python
import functools
import math

import jax
import jax.numpy as jnp
from jax import lax
from jax.experimental import pallas as pl
from jax.experimental.pallas import tpu as pltpu

D_MODEL = 1024
GLA_HEADS = 4
GLA_DV = 128
GLA_DK = 64
GLA_GATE_RANK = 16
GLA_GATE_NORMALIZER = 16.0
GLA_CHUNK = 64
GLA_WIDTH = GLA_HEADS * GLA_DV
QK_COLS = GLA_HEADS * GLA_DK
LRU_WIDTH = D_MODEL - GLA_WIDTH
LRU_BLOCKS = 8
LRU_BLOCK = LRU_WIDTH // LRU_BLOCKS
LRU_CONV = 4
LRU_C = 8.0
FFN_HIDDEN = 3 * D_MODEL
FFN_CONV = 3
EPS = 1e-6

LANES = 128
SUBLANES = 8
MXU_DIM = 256
VMEM_LIMIT_BYTES = 56 * 1024 * 1024
MIXER_VMEM_LIMIT_BYTES = 55 * 1024 * 1024

Q0, K0, V0, G0, X0 = 0, 256, 512, 1024, 1536
GATE0 = X0
D_IN = GATE0 + GLA_GATE_RANK + 2 * LRU_WIDTH
QKVG_PIECES = ((0, 512), (512, 1024), (1024, 1280), (1280, 1536))
GATE_PAD = LANES
FFN_COL_BLOCK = 512
TIME_TILE = 1024

F32 = jnp.float32
BF16 = jnp.bfloat16


def _dot(a, b):
    return jnp.dot(a, b, preferred_element_type=F32)


def _dot_nt(a, b):
    return lax.dot_general(a, b, (((1,), (1,)), ((), ())), preferred_element_type=F32)


def _dot_tn(a, b):
    return lax.dot_general(a, b, (((0,), (0,)), ((), ())), preferred_element_type=F32)


def _rmsnorm(x, g):
    return x * lax.rsqrt(jnp.mean(x * x, axis=-1, keepdims=True) + EPS) * g


def _softplus(x):
    return jnp.maximum(x, 0.0) + jnp.log(1.0 + jnp.exp(-jnp.abs(x)))


def _sigmoid(x):
    return 1.0 / (1.0 + jnp.exp(-x))


def _gelu_tanh(x):
    c = math.sqrt(2.0 / math.pi)
    hx = 0.5 * x
    return hx + hx * jnp.tanh(x * (c + (c * 0.044715) * (x * x)))


def _interleave_rows(src, slab_ref):
    n, tt, _ = slab_ref.shape
    seg = tt // SUBLANES
    for c in range(n):
        for s in range(SUBLANES):
            slab_ref[c, pl.ds(s, seg, stride=SUBLANES), :] = (
                src[s * seg:(s + 1) * seg, c * LANES:(c + 1) * LANES])


def _time_ordered_block(slab_ref, c, s):
    seg = slab_ref.shape[1] // SUBLANES
    return slab_ref[c, pl.ds(s, seg, stride=SUBLANES), :]


def _slabs_value(slab_ref, lo=0, hi=None):
    hi = slab_ref.shape[0] if hi is None else hi
    return jnp.concatenate([slab_ref[c] for c in range(lo, hi)], axis=1)


def _segment_history(tail, prev_tail):
    first = lax.broadcasted_iota(jnp.int32, tail.shape, 1) == 0
    return jnp.where(first, pltpu.roll(prev_tail, 1, axis=1), pltpu.roll(tail, 1, axis=1))


def _causal_conv_interleaved(x3, before, w_ref, b_ref, cols):
    taps = w_ref.shape[0]
    hist = taps - 1
    seg = x3.shape[0]
    out = b_ref[:, cols] + x3 * w_ref[hist:taps, cols]
    for k in range(1, taps):
        xk = jnp.concatenate([before[hist - k:], x3[:seg - k]], axis=0)
        out = out + xk * w_ref[hist - k:hist - k + 1, cols]
    return out


def _mixer_kernel(h_ref, ln_ref, win_ref, w2_ref, b2_ref, gnorm_ref,
                  cw_ref, cb_ref, wa_ref, ba_ref, wx_ref, bx_ref, lam_ref, wout_ref,
                  *refs, tt, cast_next):
    n_cast = 3 if cast_next else 2
    cast_in, out_ref, cast_out, scratch = refs[:n_cast], refs[n_cast], refs[n_cast + 1:2 * n_cast + 1], refs[2 * n_cast + 1:]
    _mixer_body(h_ref, ln_ref, win_ref, w2_ref, b2_ref, gnorm_ref,
                cw_ref, cb_ref, wa_ref, ba_ref, wx_ref, bx_ref, lam_ref, wout_ref, out_ref, *scratch, tt=tt)
    _cast_weight_rows(cast_in, cast_out)


def _cast_weight_rows(cast_in, cast_out):
    for src_ref, dst_ref in zip(cast_in, cast_out):
        dst_ref[...] = src_ref[...].astype(BF16)


def _mixer_body(h_ref, ln_ref, win_ref, w2_ref, b2_ref, gnorm_ref,
                cw_ref, cb_ref, wa_ref, ba_ref, wx_ref, bx_ref, lam_ref, wout_ref,
                out_ref,
                wxy_ref, wgate_ref, ub_ref, qkvg_ref, yg_ref, xil_ref, bnat_ref, qi_ref, ki_ref, kd_ref,
                y_ref, st_ref, a_ref, u_ref, sts_ref, xtail_ref, hs_ref, ps_ref, hil_ref, hlast_ref, *, tt):
    i = pl.program_id(0)
    seg = tt // SUBLANES

    @pl.when(i == 0)
    def _init():
        st_ref[...] = jnp.zeros_like(st_ref)
        xtail_ref[...] = jnp.zeros_like(xtail_ref)
        hlast_ref[...] = jnp.zeros_like(hlast_ref)
        wxy_ref[...] = win_ref[:, GATE0 + GLA_GATE_RANK:]
        lane = lax.broadcasted_iota(jnp.int32, wgate_ref.shape, 1)
        wgate_ref[...] = jnp.where(lane < GLA_GATE_RANK, win_ref[:, GATE0:GATE0 + GATE_PAD], jnp.zeros_like(wgate_ref))

    ub_ref[...] = _rmsnorm(h_ref[...], ln_ref[...]).astype(BF16)

    _interleave_rows(_dot(ub_ref[...], wxy_ref[:, :LRU_WIDTH]), xil_ref)
    yg_ref[...] = _dot(ub_ref[...], wxy_ref[:, LRU_WIDTH:])
    x3 = _slabs_value(xil_ref).reshape(seg, SUBLANES, LRU_WIDTH)
    tail = x3[seg - (LRU_CONV - 1):]
    before = _segment_history(tail, xtail_ref[...])
    xtail_ref[...] = tail
    xc = _causal_conv_interleaved(x3, before, cw_ref, cb_ref, slice(None)).reshape(tt, LRU_WIDTH)
    neg_c_softplus = (-LRU_C) * _softplus(-lam_ref[...])
    half = LRU_WIDTH // 2
    jb = seg // len(QKVG_PIECES)
    rb = jb * SUBLANES
    hrun = prun = None
    for blk, (c_lo, c_hi) in enumerate(QKVG_PIECES):
        xcr = xc[blk * rb:(blk + 1) * rb]
        xcb = xcr.astype(BF16)
        ra = jnp.concatenate([_dot(xcb[:, :half], wa_ref[0]), _dot(xcb[:, half:], wa_ref[1])], axis=1)
        rx = jnp.concatenate([_dot(xcb[:, :half], wx_ref[0]), _dot(xcb[:, half:], wx_ref[1])], axis=1)
        r_gate = _sigmoid(ra + ba_ref[...])
        i_gate = _sigmoid(rx + bx_ref[...])
        log_a = r_gate * neg_c_softplus
        a = jnp.exp(log_a)
        th = jnp.tanh(log_a)
        mult = jnp.sqrt((-2.0 * th) / (1.0 - th))
        if blk == 0:
            row = lax.broadcasted_iota(jnp.int32, (rb, LRU_WIDTH), 0)
            mult = jnp.where(jnp.logical_and(row == 0, i == 0), 1.0, mult)
        u3 = (mult * (i_gate * xcr)).reshape(jb, SUBLANES, LRU_WIDTH)
        a3 = a.reshape(jb, SUBLANES, LRU_WIDTH)
        for j in range(jb):
            if hrun is None:
                hrun, prun = u3[0], a3[0]
            else:
                hrun = a3[j] * hrun + u3[j]
                prun = a3[j] * prun
            hs_ref[blk * jb + j] = hrun
            ps_ref[blk * jb + j] = prun
        qkvg_ref[:, c_lo:c_hi] = _dot(ub_ref[...], win_ref[:, c_lo:c_hi])
    sub = lax.broadcasted_iota(jnp.int32, (SUBLANES, LRU_WIDTH), 0)
    h_prev = hlast_ref[...]
    cu = jnp.where(sub == 0, hrun + prun * h_prev, hrun)
    ca = prun
    d = 1
    while d < SUBLANES:
        cu = ca * jnp.where(sub >= d, pltpu.roll(cu, d, axis=0), 0.0) + cu
        if 2 * d < SUBLANES:
            ca = ca * jnp.where(sub >= d, pltpu.roll(ca, d, axis=0), 1.0)
        d *= 2
    h_enter = jnp.where(sub == 0, h_prev, pltpu.roll(cu, 1, axis=0))
    hlast_ref[...] = cu[SUBLANES - 1:]
    h2 = (hs_ref[...] + ps_ref[...] * h_enter[None]).reshape(tt, LRU_WIDTH)
    for c in range(LRU_WIDTH // LANES):
        hil_ref[c] = h2[:, c * LANES:(c + 1) * LANES]
    glr = _dot(ub_ref[...], wgate_ref[...])
    for c in range(LRU_WIDTH // LANES):
        for s in range(SUBLANES):
            rows = slice(s * seg, (s + 1) * seg)
            lanes = slice(c * LANES, (c + 1) * LANES)
            y_ref[rows, GLA_WIDTH + c * LANES:GLA_WIDTH + (c + 1) * LANES] = (
                _time_ordered_block(hil_ref, c, s) * _gelu_tanh(yg_ref[rows, lanes])).astype(BF16)
    out_ref[...] = h_ref[...] + _dot(y_ref[:, GLA_WIDTH:], wout_ref[GLA_WIDTH:, :])

    gate = _dot(glr.astype(BF16), w2_ref[...]) + b2_ref[...]
    log_alpha = -_softplus(-gate) * (1.0 / GLA_GATE_NORMALIZER)
    lail_ref = hil_ref.at[0:QK_COLS // LANES]
    bil_ref = xil_ref
    _interleave_rows(log_alpha, lail_ref)
    la3 = _slabs_value(lail_ref).reshape(seg, SUBLANES, QK_COLS)
    nq = QK_COLS // LANES
    for c0 in range(0, seg, GLA_CHUNK):
        run = la3[c0]
        cum = [run]
        for j in range(c0 + 1, c0 + GLA_CHUNK):
            run = run + la3[j]
            cum.append(run)
        for j in range(GLA_CHUNK):
            for c in range(nq):
                lanes = slice(c * LANES, (c + 1) * LANES)
                rows = slice((c0 + j) * SUBLANES, (c0 + j + 1) * SUBLANES)
                bil_ref[c, rows, :] = cum[j][:, lanes]
                bil_ref[nq + c, rows, :] = (run - cum[j])[:, lanes]
    for s in range(SUBLANES):
        rows = slice(s * seg, (s + 1) * seg)
        b = jnp.concatenate([_time_ordered_block(bil_ref, c, s) for c in range(nq)], axis=1)
        bd = jnp.concatenate([_time_ordered_block(bil_ref, nq + c, s) for c in range(nq)], axis=1)
        q = qkvg_ref[rows, Q0:Q0 + QK_COLS]
        k = qkvg_ref[rows, K0:K0 + QK_COLS]
        bnat_ref[rows, :] = b
        qi_ref[rows, :] = ((q * (GLA_DK ** -0.5)) * jnp.exp(b)).astype(BF16)
        ki_ref[rows, :] = (k * jnp.exp(-b)).astype(BF16)
        kd_ref[rows, :] = (k * jnp.exp(bd)).astype(BF16)

    lane_head = lax.broadcasted_iota(jnp.int32, (GLA_CHUNK, QK_COLS), 1) // GLA_DK
    head_masks = [lane_head == hd for hd in range(GLA_HEADS)]
    c_row = lax.broadcasted_iota(jnp.int32, (GLA_HEADS * GLA_CHUNK, GLA_CHUNK), 0) % GLA_CHUNK
    c_col = lax.broadcasted_iota(jnp.int32, (GLA_HEADS * GLA_CHUNK, GLA_CHUNK), 1)
    causal = c_row >= c_col

    def stack_heads(x):
        zero = jnp.zeros_like(x)
        return jnp.concatenate([jnp.where(m, x, zero) for m in head_masks], axis=0)

    gnorm = gnorm_ref[...]
    n_chunks = tt // GLA_CHUNK
    chunk_rows = lambda c: slice(c * GLA_CHUNK, (c + 1) * GLA_CHUNK)
    head_cols = lambda hd: slice(hd * GLA_DV, (hd + 1) * GLA_DV)

    for c in range(n_chunks):
        rows = chunk_rows(c)
        qs = stack_heads(qi_ref[rows, :])
        ks = stack_heads(kd_ref[rows, :])
        v = qkvg_ref[rows, V0:V0 + GLA_WIDTH].astype(BF16)
        a_ref[c] = jnp.where(causal, _dot_nt(qs, ki_ref[rows, :]), 0.0).astype(BF16)
        v_stack = jnp.concatenate([v[:, head_cols(hd)] for hd in range(GLA_HEADS)], axis=0)
        u_ref[c] = _dot_tn(v_stack, ks)

    st = st_ref[...]
    for c in range(n_chunks):
        sts_ref[c] = st.T.astype(BF16)
        decay = jnp.exp(bnat_ref[(c + 1) * GLA_CHUNK - 1:(c + 1) * GLA_CHUNK, :])
        st = decay * st + u_ref[c]
    st_ref[...] = st

    out_group = n_chunks // 2
    for c in range(n_chunks):
        rows = chunk_rows(c)
        qs = stack_heads(qi_ref[rows, :])
        v = qkvg_ref[rows, V0:V0 + GLA_WIDTH].astype(BF16)
        o_inter = _dot(qs, sts_ref[c])
        a_all = a_ref[c]
        for hd in range(GLA_HEADS):
            hr = slice(hd * GLA_CHUNK, (hd + 1) * GLA_CHUNK)
            vs = head_cols(hd)
            o = _dot(a_all[hr], v[:, vs]) + o_inter[hr]
            gh = qkvg_ref[rows, G0 + hd * GLA_DV:G0 + (hd + 1) * GLA_DV]
            y_ref[rows, vs] = (_rmsnorm(o, gnorm) * (gh * _sigmoid(gh))).astype(BF16)
        if (c + 1) % out_group == 0:
            orows = slice((c + 1 - out_group) * GLA_CHUNK, (c + 1) * GLA_CHUNK)
            out_ref[orows, :] = out_ref[orows, :] + _dot(y_ref[orows, :GLA_WIDTH], wout_ref[:GLA_WIDTH, :])


def _ffn_kernel(h_ref, ln_ref, win_ref, cw_ref, cb_ref, wdown_ref, lnf_ref, out_ref,
                hil_ref, oil_ref, tail_ref, act_ref, *, tt, final_norm):
    i = pl.program_id(0)
    seg = tt // SUBLANES
    hist = FFN_CONV - 1

    @pl.when(i == 0)
    def _init():
        tail_ref[...] = jnp.zeros_like(tail_ref)

    _interleave_rows(h_ref, hil_ref)
    h_in = _slabs_value(hil_ref)
    ub = _rmsnorm(h_in, ln_ref[...]).astype(BF16)

    def conv(c0):
        cols = slice(c0, c0 + FFN_COL_BLOCK)
        z = _dot(ub, win_ref[:, cols]).reshape(seg, SUBLANES, FFN_COL_BLOCK)
        tail = z[seg - hist:]
        before = _segment_history(tail, tail_ref[:, :, cols])
        tail_ref[:, :, cols] = tail
        return _causal_conv_interleaved(z, before, cw_ref, cb_ref, cols).reshape(tt, FFN_COL_BLOCK)

    for blk in range(FFN_HIDDEN // FFN_COL_BLOCK):
        c0 = blk * FFN_COL_BLOCK
        za = conv(c0)
        zg = conv(FFN_HIDDEN + c0)
        act_ref[:, c0:c0 + FFN_COL_BLOCK] = (_gelu_tanh(za) * zg).astype(BF16)

    out = h_in + _dot(act_ref[...], wdown_ref[...])
    if final_norm:
        out = _rmsnorm(out, lnf_ref[...])
    n_slabs = D_MODEL // LANES
    for c in range(n_slabs):
        oil_ref[c] = out[:, c * LANES:(c + 1) * LANES]
    for c in range(n_slabs):
        for s in range(SUBLANES):
            out_ref[s * seg:(s + 1) * seg, c * LANES:(c + 1) * LANES] = _time_ordered_block(oil_ref, c, s)


def _const_spec(shape):
    zeros = (0,) * len(shape)
    return pl.BlockSpec(shape, lambda i: zeros, pipeline_mode=pl.Buffered(1))


def _layer_spec(arr, layer):
    zeros = (0,) * (arr.ndim - 1)
    return pl.BlockSpec((None,) + arr.shape[1:], lambda i: (layer,) + zeros, pipeline_mode=pl.Buffered(1))


def _layer_rows_spec(arr, layer, n_steps):
    rows = arr.shape[1] // n_steps
    assert rows * n_steps == arr.shape[1] and rows % 16 == 0
    return pl.BlockSpec((None, rows, arr.shape[2]), lambda i: (layer, i, 0))


def _rows_out(n_rows, n_cols, n_steps):
    rows = n_rows // n_steps
    return (jax.ShapeDtypeStruct((n_rows, n_cols), BF16), pl.BlockSpec((rows, n_cols), lambda i: (i, 0)))


def _time_tile(t):
    assert t % TIME_TILE == 0 and (TIME_TILE // SUBLANES) % GLA_CHUNK == 0
    return TIME_TILE


def _mixer_call(h, layer, wout, stacked, raw):
    t = h.shape[0]
    tt = _time_tile(t)
    seg = tt // SUBLANES
    n_steps = t // tt
    depth = raw[0].shape[0]
    cast_next = layer + 1 < depth
    ffn_w_in, ffn_w_down, w_out = raw
    row_spec = pl.BlockSpec((tt, D_MODEL), lambda i: (i, 0))
    in_specs = ([row_spec] + [_layer_spec(a, layer) for a in stacked] + [_const_spec(wout.shape)]
                + [_layer_rows_spec(ffn_w_in, layer, n_steps), _layer_rows_spec(ffn_w_down, layer, n_steps)])
    args = [h, *stacked, wout, ffn_w_in, ffn_w_down]
    outs = [(jax.ShapeDtypeStruct((t, D_MODEL), F32), row_spec),
            _rows_out(D_MODEL, 2 * FFN_HIDDEN, n_steps), _rows_out(FFN_HIDDEN, D_MODEL, n_steps)]
    if cast_next:
        in_specs.append(_layer_rows_spec(w_out, layer + 1, n_steps))
        args.append(w_out)
        outs.append(_rows_out(D_MODEL, D_MODEL, n_steps))
    res = pl.pallas_call(
        functools.partial(_mixer_kernel, tt=tt, cast_next=cast_next),
        grid=(n_steps,),
        in_specs=in_specs,
        out_specs=[o[1] for o in outs],
        out_shape=[o[0] for o in outs],
        scratch_shapes=[
            pltpu.VMEM((D_MODEL, 2 * LRU_WIDTH), BF16),
            pltpu.VMEM((D_MODEL, GATE_PAD), BF16),
            pltpu.VMEM((tt, D_MODEL), BF16),
            pltpu.VMEM((tt, X0), F32),
            pltpu.VMEM((tt, LRU_WIDTH), F32),
            pltpu.VMEM((LRU_WIDTH // LANES, tt, LANES), F32),
            pltpu.VMEM((tt, QK_COLS), F32),
            pltpu.VMEM((tt, QK_COLS), BF16),
            pltpu.VMEM((tt, QK_COLS), BF16),
            pltpu.VMEM((tt, QK_COLS), BF16),
            pltpu.VMEM((tt, D_MODEL), BF16),
            pltpu.VMEM((GLA_DV, QK_COLS), F32),
            pltpu.VMEM((tt // GLA_CHUNK, GLA_HEADS * GLA_CHUNK, GLA_CHUNK), BF16),
            pltpu.VMEM((tt // GLA_CHUNK, GLA_DV, QK_COLS), F32),
            pltpu.VMEM((tt // GLA_CHUNK, QK_COLS, GLA_DV), BF16),
            pltpu.VMEM((LRU_CONV - 1, SUBLANES, LRU_WIDTH), F32),
            pltpu.VMEM((seg, SUBLANES, LRU_WIDTH), F32),
            pltpu.VMEM((seg, SUBLANES, LRU_WIDTH), F32),
            pltpu.VMEM((LRU_WIDTH // LANES, tt, LANES), F32),
            pltpu.VMEM((1, LRU_WIDTH), F32),
        ],
        compiler_params=pltpu.CompilerParams(
            dimension_semantics=("arbitrary",), vmem_limit_bytes=MIXER_VMEM_LIMIT_BYTES),
        name="mixer",
    )(*args)
    return res[0], tuple(res[1:3]), (res[3] if cast_next else None)


def _ffn_call(h, layer, ffn_w, stacked, lnf, final_norm):
    t = h.shape[0]
    tt = _time_tile(t)
    win, wdown = ffn_w
    ln, cw, cb = stacked
    row_spec = pl.BlockSpec((tt, D_MODEL), lambda i: (i, 0))
    lspec = lambda a: _layer_spec(a, layer)
    return pl.pallas_call(
        functools.partial(_ffn_kernel, tt=tt, final_norm=final_norm),
        grid=(t // tt,),
        in_specs=[row_spec, lspec(ln), _const_spec(win.shape), lspec(cw), lspec(cb), _const_spec(wdown.shape),
                  _const_spec(lnf.shape)],
        out_specs=row_spec,
        out_shape=jax.ShapeDtypeStruct((t, D_MODEL), F32),
        scratch_shapes=[
            pltpu.VMEM((D_MODEL // LANES, tt, LANES), F32),
            pltpu.VMEM((D_MODEL // LANES, tt, LANES), F32),
            pltpu.VMEM((FFN_CONV - 1, SUBLANES, 2 * FFN_HIDDEN), F32),
            pltpu.VMEM((tt, FFN_HIDDEN), BF16),
        ],
        compiler_params=pltpu.CompilerParams(
            dimension_semantics=("arbitrary",), vmem_limit_bytes=VMEM_LIMIT_BYTES),
        name="ffn",
    )(h, ln, win, cw, cb, wdown, lnf)


def _block_diag_tiles(w):
    depth = w.shape[0]
    per = MXU_DIM // LRU_BLOCK
    w5 = w.reshape(depth, LRU_BLOCKS // per, per, LRU_BLOCK, LRU_BLOCK)
    tiles = jnp.einsum("lsjab,jk->lsjakb", w5, jnp.eye(per, dtype=w.dtype))
    return tiles.reshape(depth, LRU_BLOCKS // per, MXU_DIM, MXU_DIM).astype(BF16)


def kernel(x, ln_mix, w_in, gla_gate_w2, gla_gate_b, gla_norm, lru_conv_w, lru_conv_b,
           lru_wa, lru_ba, lru_wx, lru_bx, lru_lambda, w_out, ln_ffn, ffn_w_in,
           ffn_conv_w, ffn_conv_b, ffn_w_down, ln_final):
    bsz, t, d = x.shape
    assert bsz == 1 and d == D_MODEL and w_in.shape[2] == D_IN
    depth = w_in.shape[0]
    h = x.reshape(t, d)
    rows = lambda p: p.reshape(depth, 1, -1)
    w2 = jnp.pad(gla_gate_w2, ((0, 0), (0, GATE_PAD - GLA_GATE_RANK), (0, 0))).astype(BF16)
    mixer_stacked = (rows(ln_mix), w_in.astype(BF16), w2, rows(gla_gate_b), rows(gla_norm), lru_conv_w,
                     rows(lru_conv_b), _block_diag_tiles(lru_wa), rows(lru_ba), _block_diag_tiles(lru_wx),
                     rows(lru_bx), rows(lru_lambda))
    ffn_small = (rows(ln_ffn), ffn_conv_w, rows(ffn_conv_b))
    raw = (ffn_w_in, ffn_w_down, w_out)
    wout = w_out[0].astype(BF16)
    for l in range(depth):
        h, ffn_w, wout = _mixer_call(h, l, wout, mixer_stacked, raw)
        h = _ffn_call(h, l, ffn_w, ffn_small, ln_final.reshape(1, -1), final_norm=(l == depth - 1))
    return h.reshape(bsz, t, d)
```

```python
import functools
import math

import jax
import jax.numpy as jnp
from jax import lax
from jax.experimental import pallas as pl
from jax.experimental.pallas import tpu as pltpu

D_MODEL = 1024
GLA_HEADS = 4
GLA_DV = 128
GLA_DK = 64
GLA_GATE_RANK = 16
GLA_GATE_NORMALIZER = 16.0
GLA_CHUNK = 64
GLA_WIDTH = GLA_HEADS * GLA_DV
QK_COLS = GLA_HEADS * GLA_DK
LRU_WIDTH = D_MODEL - GLA_WIDTH
LRU_BLOCKS = 8
LRU_BLOCK = LRU_WIDTH // LRU_BLOCKS
LRU_CONV = 4
LRU_C = 8.0
FFN_HIDDEN = 3 * D_MODEL
FFN_CONV = 3
EPS = 1e-6

LANES = 128
SUBLANES = 8
MXU_DIM = 256
VMEM_LIMIT_BYTES = 56 * 1024 * 1024
MIXER_VMEM_LIMIT_BYTES = 55 * 1024 * 1024

Q0, K0, V0, G0, X0 = 0, 256, 512, 1024, 1536
GATE0 = X0
D_IN = GATE0 + GLA_GATE_RANK + 2 * LRU_WIDTH
QKVG_PIECES = ((0, 512), (512, 1024), (1024, 1280), (1280, 1536))
GATE_PAD = LANES
FFN_COL_BLOCK = 256
TIME_TILE = 1024

F32 = jnp.float32
BF16 = jnp.bfloat16


def _dot(a, b):
    return jnp.dot(a, b, preferred_element_type=F32)


def _dot_nt(a, b):
    return lax.dot_general(a, b, (((1,), (1,)), ((), ())), preferred_element_type=F32)


def _dot_tn(a, b):
    return lax.dot_general(a, b, (((0,), (0,)), ((), ())), preferred_element_type=F32)


def _rmsnorm(x, g):
    return x * lax.rsqrt(jnp.mean(x * x, axis=-1, keepdims=True) + EPS) * g


def _softplus(x):
    return jnp.maximum(x, 0.0) + jnp.log(1.0 + jnp.exp(-jnp.abs(x)))


def _sigmoid(x):
    return 1.0 / (1.0 + jnp.exp(-x))


def _gelu_tanh(x):
    c = math.sqrt(2.0 / math.pi)
    hx = 0.5 * x
    return hx + hx * jnp.tanh(x * (c + (c * 0.044715) * (x * x)))


def _interleave_rows(src, slab_ref):
    n, tt, _ = slab_ref.shape
    seg = tt // SUBLANES
    for c in range(n):
        for s in range(SUBLANES):
            slab_ref[c, pl.ds(s, seg, stride=SUBLANES), :] = (
                src[s * seg:(s + 1) * seg, c * LANES:(c + 1) * LANES])


def _time_ordered_block(slab_ref, c, s):
    seg = slab_ref.shape[1] // SUBLANES
    return slab_ref[c, pl.ds(s, seg, stride=SUBLANES), :]


def _slabs_value(slab_ref):
    return jnp.concatenate([slab_ref[c] for c in range(slab_ref.shape[0])], axis=1)


def _segment_history(tail, prev_tail):
    first = lax.broadcasted_iota(jnp.int32, tail.shape, 1) == 0
    return jnp.where(first, pltpu.roll(prev_tail, 1, axis=1), pltpu.roll(tail, 1, axis=1))


def _causal_conv_interleaved(x3, before, w_ref, b_ref, cols):
    taps = w_ref.shape[0]
    hist = taps - 1
    seg = x3.shape[0]
    out = b_ref[:, cols] + x3 * w_ref[hist:taps, cols]
    for k in range(1, taps):
        xk = jnp.concatenate([before[hist - k:], x3[:seg - k]], axis=0)
        out = out + xk * w_ref[hist - k:hist - k + 1, cols]
    return out


def _mixer_kernel(h_ref, ln_ref, win_ref, w2_ref, b2_ref, gnorm_ref,
                  cw_ref, cb_ref, wa_ref, ba_ref, wx_ref, bx_ref, lam_ref, wout_ref,
                  *refs, tt, cast_next):
    n_cast = 3 if cast_next else 2
    cast_in, out_ref, cast_out, scratch = refs[:n_cast], refs[n_cast], refs[n_cast + 1:2 * n_cast + 1], refs[2 * n_cast + 1:]
    _mixer_body(h_ref, ln_ref, win_ref, w2_ref, b2_ref, gnorm_ref,
                cw_ref, cb_ref, wa_ref, ba_ref, wx_ref, bx_ref, lam_ref, wout_ref, out_ref, *scratch, tt=tt)
    _cast_weight_rows(cast_in, cast_out)


def _cast_weight_rows(cast_in, cast_out):
    for src_ref, dst_ref in zip(cast_in, cast_out):
        dst_ref[...] = src_ref[...].astype(BF16)


def _mixer_body(h_ref, ln_ref, win_ref, w2_ref, b2_ref, gnorm_ref,
                cw_ref, cb_ref, wa_ref, ba_ref, wx_ref, bx_ref, lam_ref, wout_ref,
                out_ref,
                wxy_ref, wgate_ref, ub_ref, qkvg_ref, yg_ref, xil_ref, bnat_ref, qi_ref, ki_ref, kd_ref,
                y_ref, st_ref, a_ref, u_ref, sts_ref, xtail_ref, hs_ref, ps_ref, hil_ref, hlast_ref, *, tt):
    i = pl.program_id(0)
    seg = tt // SUBLANES

    @pl.when(i == 0)
    def _init():
        st_ref[...] = jnp.zeros_like(st_ref)
        xtail_ref[...] = jnp.zeros_like(xtail_ref)
        hlast_ref[...] = jnp.zeros_like(hlast_ref)
        wxy_ref[...] = win_ref[:, GATE0 + GLA_GATE_RANK:]
        lane = lax.broadcasted_iota(jnp.int32, wgate_ref.shape, 1)
        wgate_ref[...] = jnp.where(lane < GLA_GATE_RANK, win_ref[:, GATE0:GATE0 + GATE_PAD], jnp.zeros_like(wgate_ref))

    ub_ref[...] = _rmsnorm(h_ref[...], ln_ref[...]).astype(BF16)

    _interleave_rows(_dot(ub_ref[...], wxy_ref[:, :LRU_WIDTH]), xil_ref)
    yg_ref[...] = _dot(ub_ref[...], wxy_ref[:, LRU_WIDTH:])
    x3 = _slabs_value(xil_ref).reshape(seg, SUBLANES, LRU_WIDTH)
    tail = x3[seg - (LRU_CONV - 1):]
    before = _segment_history(tail, xtail_ref[...])
    xtail_ref[...] = tail
    xc = _causal_conv_interleaved(x3, before, cw_ref, cb_ref, slice(None)).reshape(tt, LRU_WIDTH)
    neg_c_softplus = (-LRU_C) * _softplus(-lam_ref[...])
    half = LRU_WIDTH // 2
    jb = seg // len(QKVG_PIECES)
    rb = jb * SUBLANES
    hrun = prun = None
    for blk, (c_lo, c_hi) in enumerate(QKVG_PIECES):
        xcr = xc[blk * rb:(blk + 1) * rb]
        xcb = xcr.astype(BF16)
        ra = jnp.concatenate([_dot(xcb[:, :half], wa_ref[0]), _dot(xcb[:, half:], wa_ref[1])], axis=1)
        rx = jnp.concatenate([_dot(xcb[:, :half], wx_ref[0]), _dot(xcb[:, half:], wx_ref[1])], axis=1)
        r_gate = _sigmoid(ra + ba_ref[...])
        i_gate = _sigmoid(rx + bx_ref[...])
        log_a = r_gate * neg_c_softplus
        a = jnp.exp(log_a)
        th = jnp.tanh(log_a)
        mult = jnp.sqrt((-2.0 * th) / (1.0 - th))
        if blk == 0:
            row = lax.broadcasted_iota(jnp.int32, (rb, LRU_WIDTH), 0)
            mult = jnp.where(jnp.logical_and(row == 0, i == 0), 1.0, mult)
        u3 = (mult * (i_gate * xcr)).reshape(jb, SUBLANES, LRU_WIDTH)
        a3 = a.reshape(jb, SUBLANES, LRU_WIDTH)
        for j in range(jb):
            if hrun is None:
                hrun, prun = u3[0], a3[0]
            else:
                hrun = a3[j] * hrun + u3[j]
                prun = a3[j] * prun
            hs_ref[blk * jb + j] = hrun
            ps_ref[blk * jb + j] = prun
        qkvg_ref[:, c_lo:c_hi] = _dot(ub_ref[...], win_ref[:, c_lo:c_hi])
    sub = lax.broadcasted_iota(jnp.int32, (SUBLANES, LRU_WIDTH), 0)
    h_prev = hlast_ref[...]
    cu = jnp.where(sub == 0, hrun + prun * h_prev, hrun)
    ca = prun
    d = 1
    while d < SUBLANES:
        cu = ca * jnp.where(sub >= d, pltpu.roll(cu, d, axis=0), 0.0) + cu
        if 2 * d < SUBLANES:
            ca = ca * jnp.where(sub >= d, pltpu.roll(ca, d, axis=0), 1.0)
        d *= 2
    h_enter = jnp.where(sub == 0, h_prev, pltpu.roll(cu, 1, axis=0))
    hlast_ref[...] = cu[SUBLANES - 1:]
    h2 = (hs_ref[...] + ps_ref[...] * h_enter[None]).reshape(tt, LRU_WIDTH)
    for c in range(LRU_WIDTH // LANES):
        hil_ref[c] = h2[:, c * LANES:(c + 1) * LANES]
    glr = _dot(ub_ref[...], wgate_ref[...])
    for c in range(LRU_WIDTH // LANES):
        for s in range(SUBLANES):
            rows = slice(s * seg, (s + 1) * seg)
            lanes = slice(c * LANES, (c + 1) * LANES)
            y_ref[rows, GLA_WIDTH + c * LANES:GLA_WIDTH + (c + 1) * LANES] = (
                _time_ordered_block(hil_ref, c, s) * _gelu_tanh(yg_ref[rows, lanes])).astype(BF16)
    out_ref[...] = h_ref[...] + _dot(y_ref[:, GLA_WIDTH:], wout_ref[GLA_WIDTH:, :])

    gate = _dot(glr.astype(BF16), w2_ref[...]) + b2_ref[...]
    log_alpha = -_softplus(-gate) * (1.0 / GLA_GATE_NORMALIZER)
    lail_ref = hil_ref.at[0:QK_COLS // LANES]
    bil_ref = xil_ref
    _interleave_rows(log_alpha, lail_ref)
    la3 = _slabs_value(lail_ref).reshape(seg, SUBLANES, QK_COLS)
    nq = QK_COLS // LANES
    for c0 in range(0, seg, GLA_CHUNK):
        run = la3[c0]
        cum = [run]
        for j in range(c0 + 1, c0 + GLA_CHUNK):
            run = run + la3[j]
            cum.append(run)
        for j in range(GLA_CHUNK):
            for c in range(nq):
                lanes = slice(c * LANES, (c + 1) * LANES)
                rows = slice((c0 + j) * SUBLANES, (c0 + j + 1) * SUBLANES)
                bil_ref[c, rows, :] = cum[j][:, lanes]
                bil_ref[nq + c, rows, :] = (run - cum[j])[:, lanes]
    for s in range(SUBLANES):
        rows = slice(s * seg, (s + 1) * seg)
        b = jnp.concatenate([_time_ordered_block(bil_ref, c, s) for c in range(nq)], axis=1)
        bd = jnp.concatenate([_time_ordered_block(bil_ref, nq + c, s) for c in range(nq)], axis=1)
        q = qkvg_ref[rows, Q0:Q0 + QK_COLS]
        k = qkvg_ref[rows, K0:K0 + QK_COLS]
        bnat_ref[rows, :] = b
        qi_ref[rows, :] = ((q * (GLA_DK ** -0.5)) * jnp.exp(b)).astype(BF16)
        ki_ref[rows, :] = (k * jnp.exp(-b)).astype(BF16)
        kd_ref[rows, :] = (k * jnp.exp(bd)).astype(BF16)

    lane_head = lax.broadcasted_iota(jnp.int32, (GLA_CHUNK, QK_COLS), 1) // GLA_DK
    head_masks = [lane_head == hd for hd in range(GLA_HEADS)]
    c_row = lax.broadcasted_iota(jnp.int32, (GLA_HEADS * GLA_CHUNK, GLA_CHUNK), 0) % GLA_CHUNK
    c_col = lax.broadcasted_iota(jnp.int32, (GLA_HEADS * GLA_CHUNK, GLA_CHUNK), 1)
    causal = c_row >= c_col

    def stack_heads(x):
        zero = jnp.zeros_like(x)
        return jnp.concatenate([jnp.where(m, x, zero) for m in head_masks], axis=0)

    gnorm = gnorm_ref[...]
    n_chunks = tt // GLA_CHUNK
    chunk_rows = lambda c: slice(c * GLA_CHUNK, (c + 1) * GLA_CHUNK)
    head_cols = lambda hd: slice(hd * GLA_DV, (hd + 1) * GLA_DV)

    for c in range(n_chunks):
        rows = chunk_rows(c)
        qs = stack_heads(qi_ref[rows, :])
        ks = stack_heads(kd_ref[rows, :])
        v = qkvg_ref[rows, V0:V0 + GLA_WIDTH].astype(BF16)
        a_ref[c] = jnp.where(causal, _dot_nt(qs, ki_ref[rows, :]), 0.0).astype(BF16)
        v_stack = jnp.concatenate([v[:, head_cols(hd)] for hd in range(GLA_HEADS)], axis=0)
        u_ref[c] = _dot_tn(v_stack, ks)

    st = st_ref[...]
    for c in range(n_chunks):
        sts_ref[c] = st.T.astype(BF16)
        decay = jnp.exp(bnat_ref[(c + 1) * GLA_CHUNK - 1:(c + 1) * GLA_CHUNK, :])
        st = decay * st + u_ref[c]
    st_ref[...] = st

    out_group = n_chunks // 2
    for c in range(n_chunks):
        rows = chunk_rows(c)
        qs = stack_heads(qi_ref[rows, :])
        v = qkvg_ref[rows, V0:V0 + GLA_WIDTH].astype(BF16)
        o_inter = _dot(qs, sts_ref[c])
        a_all = a_ref[c]
        for hd in range(GLA_HEADS):
            hr = slice(hd * GLA_CHUNK, (hd + 1) * GLA_CHUNK)
            vs = head_cols(hd)
            o = _dot(a_all[hr], v[:, vs]) + o_inter[hr]
            gh = qkvg_ref[rows, G0 + hd * GLA_DV:G0 + (hd + 1) * GLA_DV]
            y_ref[rows, vs] = (_rmsnorm(o, gnorm) * (gh * _sigmoid(gh))).astype(BF16)
        if (c + 1) % out_group == 0:
            orows = slice((c + 1 - out_group) * GLA_CHUNK, (c + 1) * GLA_CHUNK)
            out_ref[orows, :] = out_ref[orows, :] + _dot(y_ref[orows, :GLA_WIDTH], wout_ref[:GLA_WIDTH, :])


def _ffn_kernel(h_ref, ln_ref, win_ref, cw_ref, cb_ref, wdown_ref, lnf_ref, out_ref,
                hil_ref, oil_ref, tail_ref, act_ref, *, tt, final_norm):
    i = pl.program_id(0)
    seg = tt // SUBLANES
    hist = FFN_CONV - 1

    @pl.when(i == 0)
    def _init():
        tail_ref[...] = jnp.zeros_like(tail_ref)

    _interleave_rows(h_ref, hil_ref)
    h_in = _slabs_value(hil_ref)
    ub = _rmsnorm(h_in, ln_ref[...]).astype(BF16)

    def conv(c0):
        cols = slice(c0, c0 + FFN_COL_BLOCK)
        z = _dot(ub, win_ref[:, cols]).reshape(seg, SUBLANES, FFN_COL_BLOCK)
        tail = z[seg - hist:]
        before = _segment_history(tail, tail_ref[:, :, cols])
        tail_ref[:, :, cols] = tail
        return _causal_conv_interleaved(z, before, cw_ref, cb_ref, cols).reshape(tt, FFN_COL_BLOCK)

    for blk in range(FFN_HIDDEN // FFN_COL_BLOCK):
        c0 = blk * FFN_COL_BLOCK
        za = conv(c0)
        zg = conv(FFN_HIDDEN + c0)
        act_ref[:, c0:c0 + FFN_COL_BLOCK] = (_gelu_tanh(za) * zg).astype(BF16)

    out = h_in + _dot(act_ref[...], wdown_ref[...])
    if final_norm:
        out = _rmsnorm(out, lnf_ref[...])
    n_slabs = D_MODEL // LANES
    for c in range(n_slabs):
        oil_ref[c] = out[:, c * LANES:(c + 1) * LANES]
    for c in range(n_slabs):
        for s in range(SUBLANES):
            out_ref[s * seg:(s + 1) * seg, c * LANES:(c + 1) * LANES] = _time_ordered_block(oil_ref, c, s)


def _const_spec(shape):
    zeros = (0,) * len(shape)
    return pl.BlockSpec(shape, lambda i: zeros, pipeline_mode=pl.Buffered(1))


def _layer_spec(arr, layer):
    zeros = (0,) * (arr.ndim - 1)
    return pl.BlockSpec((None,) + arr.shape[1:], lambda i: (layer,) + zeros, pipeline_mode=pl.Buffered(1))


def _layer_rows_spec(arr, layer, n_steps):
    rows = arr.shape[1] // n_steps
    assert rows * n_steps == arr.shape[1] and rows % 16 == 0
    return pl.BlockSpec((None, rows, arr.shape[2]), lambda i: (layer, i, 0))


def _rows_out(n_rows, n_cols, n_steps):
    rows = n_rows // n_steps
    return (jax.ShapeDtypeStruct((n_rows, n_cols), BF16), pl.BlockSpec((rows, n_cols), lambda i: (i, 0)))


def _time_tile(t):
    assert t % TIME_TILE == 0 and (TIME_TILE // SUBLANES) % GLA_CHUNK == 0
    return TIME_TILE


def _mixer_call(h, layer, wout, stacked, raw):
    t = h.shape[0]
    tt = _time_tile(t)
    seg = tt // SUBLANES
    n_steps = t // tt
    depth = raw[0].shape[0]
    cast_next = layer + 1 < depth
    ffn_w_in, ffn_w_down, w_out = raw
    row_spec = pl.BlockSpec((tt, D_MODEL), lambda i: (i, 0))
    in_specs = ([row_spec] + [_layer_spec(a, layer) for a in stacked] + [_const_spec(wout.shape)]
                + [_layer_rows_spec(ffn_w_in, layer, n_steps), _layer_rows_spec(ffn_w_down, layer, n_steps)])
    args = [h, *stacked, wout, ffn_w_in, ffn_w_down]
    outs = [(jax.ShapeDtypeStruct((t, D_MODEL), F32), row_spec),
            _rows_out(D_MODEL, 2 * FFN_HIDDEN, n_steps), _rows_out(FFN_HIDDEN, D_MODEL, n_steps)]
    if cast_next:
        in_specs.append(_layer_rows_spec(w_out, layer + 1, n_steps))
        args.append(w_out)
        outs.append(_rows_out(D_MODEL, D_MODEL, n_steps))
    res = pl.pallas_call(
        functools.partial(_mixer_kernel, tt=tt, cast_next=cast_next),
        grid=(n_steps,),
        in_specs=in_specs,
        out_specs=[o[1] for o in outs],
        out_shape=[o[0] for o in outs],
        scratch_shapes=[
            pltpu.VMEM((D_MODEL, 2 * LRU_WIDTH), BF16),
            pltpu.VMEM((D_MODEL, GATE_PAD), BF16),
            pltpu.VMEM((tt, D_MODEL), BF16),
            pltpu.VMEM((tt, X0), F32),
            pltpu.VMEM((tt, LRU_WIDTH), F32),
            pltpu.VMEM((LRU_WIDTH // LANES, tt, LANES), F32),
            pltpu.VMEM((tt, QK_COLS), F32),
            pltpu.VMEM((tt, QK_COLS), BF16),
            pltpu.VMEM((tt, QK_COLS), BF16),
            pltpu.VMEM((tt, QK_COLS), BF16),
            pltpu.VMEM((tt, D_MODEL), BF16),
            pltpu.VMEM((GLA_DV, QK_COLS), F32),
            pltpu.VMEM((tt // GLA_CHUNK, GLA_HEADS * GLA_CHUNK, GLA_CHUNK), BF16),
            pltpu.VMEM((tt // GLA_CHUNK, GLA_DV, QK_COLS), F32),
            pltpu.VMEM((tt // GLA_CHUNK, QK_COLS, GLA_DV), BF16),
            pltpu.VMEM((LRU_CONV - 1, SUBLANES, LRU_WIDTH), F32),
            pltpu.VMEM((seg, SUBLANES, LRU_WIDTH), F32),
            pltpu.VMEM((seg, SUBLANES, LRU_WIDTH), F32),
            pltpu.VMEM((LRU_WIDTH // LANES, tt, LANES), F32),
            pltpu.VMEM((1, LRU_WIDTH), F32),
        ],
        compiler_params=pltpu.CompilerParams(
            dimension_semantics=("arbitrary",), vmem_limit_bytes=MIXER_VMEM_LIMIT_BYTES),
        name="mixer",
    )(*args)
    return res[0], tuple(res[1:3]), (res[3] if cast_next else None)


def _ffn_call(h, layer, ffn_w, stacked, lnf, final_norm):
    t = h.shape[0]
    tt = _time_tile(t)
    win, wdown = ffn_w
    ln, cw, cb = stacked
    row_spec = pl.BlockSpec((tt, D_MODEL), lambda i: (i, 0))
    lspec = lambda a: _layer_spec(a, layer)
    return pl.pallas_call(
        functools.partial(_ffn_kernel, tt=tt, final_norm=final_norm),
        grid=(t // tt,),
        in_specs=[row_spec, lspec(ln), _const_spec(win.shape), lspec(cw), lspec(cb), _const_spec(wdown.shape),
                  _const_spec(lnf.shape)],
        out_specs=row_spec,
        out_shape=jax.ShapeDtypeStruct((t, D_MODEL), F32),
        scratch_shapes=[
            pltpu.VMEM((D_MODEL // LANES, tt, LANES), F32),
            pltpu.VMEM((D_MODEL // LANES, tt, LANES), F32),
            pltpu.VMEM((FFN_CONV - 1, SUBLANES, 2 * FFN_HIDDEN), F32),
            pltpu.VMEM((tt, FFN_HIDDEN), BF16),
        ],
        compiler_params=pltpu.CompilerParams(
            dimension_semantics=("arbitrary",), vmem_limit_bytes=VMEM_LIMIT_BYTES),
        name="ffn",
    )(h, ln, win, cw, cb, wdown, lnf)


def _block_diag_tiles(w):
    depth = w.shape[0]
    per = MXU_DIM // LRU_BLOCK
    w5 = w.reshape(depth, LRU_BLOCKS // per, per, LRU_BLOCK, LRU_BLOCK)
    tiles = jnp.einsum("lsjab,jk->lsjakb", w5, jnp.eye(per, dtype=w.dtype))
    return tiles.reshape(depth, LRU_BLOCKS // per, MXU_DIM, MXU_DIM).astype(BF16)


def kernel(x, ln_mix, w_in, gla_gate_w2, gla_gate_b, gla_norm, lru_conv_w, lru_conv_b,
           lru_wa, lru_ba, lru_wx, lru_bx, lru_lambda, w_out, ln_ffn, ffn_w_in,
           ffn_conv_w, ffn_conv_b, ffn_w_down, ln_final):
    bsz, t, d = x.shape
    assert bsz == 1 and d == D_MODEL and w_in.shape[2] == D_IN
    depth = w_in.shape[0]
    h = x.reshape(t, d)
    rows = lambda p: p.reshape(depth, 1, -1)
    w2 = jnp.pad(gla_gate_w2, ((0, 0), (0, GATE_PAD - GLA_GATE_RANK), (0, 0))).astype(BF16)
    mixer_stacked = (rows(ln_mix), w_in.astype(BF16), w2, rows(gla_gate_b), rows(gla_norm), lru_conv_w,
                     rows(lru_conv_b), _block_diag_tiles(lru_wa), rows(lru_ba), _block_diag_tiles(lru_wx),
                     rows(lru_bx), rows(lru_lambda))
    ffn_small = (rows(ln_ffn), ffn_conv_w, rows(ffn_conv_b))
    raw = (ffn_w_in, ffn_w_down, w_out)
    wout = w_out[0].astype(BF16)
    for l in range(depth):
        h, ffn_w, wout = _mixer_call(h, l, wout, mixer_stacked, raw)
        h = _ffn_call(h, l, ffn_w, ffn_small, ln_final.reshape(1, -1), final_norm=(l == depth - 1))
    return h.reshape(bsz, t, d)
```

```python
import functools
import math

import jax
import jax.numpy as jnp
from jax import lax
from jax.experimental import pallas as pl
from jax.experimental.pallas import tpu as pltpu

D_MODEL = 1024
GLA_HEADS = 4
GLA_DV = 128
GLA_DK = 64
GLA_GATE_RANK = 16
GLA_GATE_NORMALIZER = 16.0
GLA_CHUNK = 64
GLA_WIDTH = GLA_HEADS * GLA_DV
QK_COLS = GLA_HEADS * GLA_DK
LRU_WIDTH = D_MODEL - GLA_WIDTH
LRU_BLOCKS = 8
LRU_BLOCK = LRU_WIDTH // LRU_BLOCKS
LRU_CONV = 4
LRU_C = 8.0
FFN_HIDDEN = 3 * D_MODEL
FFN_CONV = 3
EPS = 1e-6

LANES = 128
SUBLANES = 8
MXU_DIM = 256
VMEM_LIMIT_BYTES = 56 * 1024 * 1024
MIXER_VMEM_LIMIT_BYTES = 55 * 1024 * 1024

Q0, K0, V0, G0, X0 = 0, 256, 512, 1024, 1536
GATE0 = X0
D_IN = GATE0 + GLA_GATE_RANK + 2 * LRU_WIDTH
QKVG_PIECES = ((0, 512), (512, 1024), (1024, 1280), (1280, 1536))
GATE_PAD = LANES
NORM_ROWS = 128
FFN_COL_BLOCK = 256
TIME_TILE = 1024

F32 = jnp.float32
BF16 = jnp.bfloat16


def _dot(a, b):
    return jnp.dot(a, b, preferred_element_type=F32)


def _dot_nt(a, b):
    return lax.dot_general(a, b, (((1,), (1,)), ((), ())), preferred_element_type=F32)


def _dot_tn(a, b):
    return lax.dot_general(a, b, (((0,), (0,)), ((), ())), preferred_element_type=F32)


def _rmsnorm(x, g):
    return x * lax.rsqrt(jnp.mean(x * x, axis=-1, keepdims=True) + EPS) * g


def _softplus(x):
    return jnp.maximum(x, 0.0) + jnp.log(1.0 + jnp.exp(-jnp.abs(x)))


def _sigmoid(x):
    return 1.0 / (1.0 + jnp.exp(-x))


def _gelu_tanh(x):
    c = math.sqrt(2.0 / math.pi)
    hx = 0.5 * x
    return hx + hx * jnp.tanh(x * (c + (c * 0.044715) * (x * x)))


def _interleave_rows(src, slab_ref):
    n, tt, _ = slab_ref.shape
    seg = tt // SUBLANES
    for c in range(n):
        for s in range(SUBLANES):
            slab_ref[c, pl.ds(s, seg, stride=SUBLANES), :] = (
                src[s * seg:(s + 1) * seg, c * LANES:(c + 1) * LANES])


def _time_ordered_block(slab_ref, c, s):
    seg = slab_ref.shape[1] // SUBLANES
    return slab_ref[c, pl.ds(s, seg, stride=SUBLANES), :]


def _slabs_value(slab_ref):
    return jnp.concatenate([slab_ref[c] for c in range(slab_ref.shape[0])], axis=1)


def _segment_history(tail, prev_tail):
    first = lax.broadcasted_iota(jnp.int32, tail.shape, 1) == 0
    return jnp.where(first, pltpu.roll(prev_tail, 1, axis=1), pltpu.roll(tail, 1, axis=1))


def _causal_conv_interleaved(x3, before, w_ref, b_ref, cols):
    taps = w_ref.shape[0]
    hist = taps - 1
    seg = x3.shape[0]
    out = b_ref[:, cols] + x3 * w_ref[hist:taps, cols]
    for k in range(1, taps):
        xk = jnp.concatenate([before[hist - k:], x3[:seg - k]], axis=0)
        out = out + xk * w_ref[hist - k:hist - k + 1, cols]
    return out


def _mixer_kernel(h_ref, ln_ref, win_ref, w2_ref, b2_ref, gnorm_ref,
                  cw_ref, cb_ref, wa_ref, ba_ref, wx_ref, bx_ref, lam_ref, wout_ref,
                  *refs, tt, cast_next):
    n_cast = 3 if cast_next else 2
    cast_in, out_ref, cast_out, scratch = refs[:n_cast], refs[n_cast], refs[n_cast + 1:2 * n_cast + 1], refs[2 * n_cast + 1:]
    _mixer_body(h_ref, ln_ref, win_ref, w2_ref, b2_ref, gnorm_ref,
                cw_ref, cb_ref, wa_ref, ba_ref, wx_ref, bx_ref, lam_ref, wout_ref, out_ref, *scratch, tt=tt)
    _cast_weight_rows(cast_in, cast_out)


def _cast_weight_rows(cast_in, cast_out):
    for src_ref, dst_ref in zip(cast_in, cast_out):
        dst_ref[...] = src_ref[...].astype(BF16)


def _mixer_body(h_ref, ln_ref, win_ref, w2_ref, b2_ref, gnorm_ref,
                cw_ref, cb_ref, wa_ref, ba_ref, wx_ref, bx_ref, lam_ref, wout_ref,
                out_ref,
                wxy_ref, wgate_ref, ub_ref, qkvg_ref, yg_ref, xil_ref, bnat_ref, qi_ref, ki_ref, kd_ref,
                y_ref, st_ref, a_ref, u_ref, sts_ref, xtail_ref, hs_ref, ps_ref, hil_ref, hlast_ref, *, tt):
    i = pl.program_id(0)
    seg = tt // SUBLANES

    @pl.when(i == 0)
    def _init():
        st_ref[...] = jnp.zeros_like(st_ref)
        xtail_ref[...] = jnp.zeros_like(xtail_ref)
        hlast_ref[...] = jnp.zeros_like(hlast_ref)
        wxy_ref[...] = win_ref[:, GATE0 + GLA_GATE_RANK:]
        lane = lax.broadcasted_iota(jnp.int32, wgate_ref.shape, 1)
        wgate_ref[...] = jnp.where(lane < GLA_GATE_RANK, win_ref[:, GATE0:GATE0 + GATE_PAD], jnp.zeros_like(wgate_ref))

    ub_ref[...] = _rmsnorm(h_ref[...], ln_ref[...]).astype(BF16)

    _interleave_rows(_dot(ub_ref[...], wxy_ref[:, :LRU_WIDTH]), xil_ref)
    yg_ref[...] = _dot(ub_ref[...], wxy_ref[:, LRU_WIDTH:])
    x3 = _slabs_value(xil_ref).reshape(seg, SUBLANES, LRU_WIDTH)
    tail = x3[seg - (LRU_CONV - 1):]
    before = _segment_history(tail, xtail_ref[...])
    xtail_ref[...] = tail
    xc = _causal_conv_interleaved(x3, before, cw_ref, cb_ref, slice(None)).reshape(tt, LRU_WIDTH)
    neg_c_softplus = (-LRU_C) * _softplus(-lam_ref[...])
    half = LRU_WIDTH // 2
    jb = seg // len(QKVG_PIECES)
    rb = jb * SUBLANES
    hrun = prun = None
    for blk, (c_lo, c_hi) in enumerate(QKVG_PIECES):
        xcr = xc[blk * rb:(blk + 1) * rb]
        xcb = xcr.astype(BF16)
        ra = jnp.concatenate([_dot(xcb[:, :half], wa_ref[0]), _dot(xcb[:, half:], wa_ref[1])], axis=1)
        rx = jnp.concatenate([_dot(xcb[:, :half], wx_ref[0]), _dot(xcb[:, half:], wx_ref[1])], axis=1)
        r_gate = _sigmoid(ra + ba_ref[...])
        i_gate = _sigmoid(rx + bx_ref[...])
        log_a = r_gate * neg_c_softplus
        a = jnp.exp(log_a)
        th = jnp.tanh(log_a)
        mult = jnp.sqrt((-2.0 * th) / (1.0 - th))
        if blk == 0:
            row = lax.broadcasted_iota(jnp.int32, (rb, LRU_WIDTH), 0)
            mult = jnp.where(jnp.logical_and(row == 0, i == 0), 1.0, mult)
        u3 = (mult * (i_gate * xcr)).reshape(jb, SUBLANES, LRU_WIDTH)
        a3 = a.reshape(jb, SUBLANES, LRU_WIDTH)
        for j in range(jb):
            if hrun is None:
                hrun, prun = u3[0], a3[0]
            else:
                hrun = a3[j] * hrun + u3[j]
                prun = a3[j] * prun
            hs_ref[blk * jb + j] = hrun
            ps_ref[blk * jb + j] = prun
        qkvg_ref[:, c_lo:c_hi] = _dot(ub_ref[...], win_ref[:, c_lo:c_hi])
    sub = lax.broadcasted_iota(jnp.int32, (SUBLANES, LRU_WIDTH), 0)
    h_prev = hlast_ref[...]
    cu = jnp.where(sub == 0, hrun + prun * h_prev, hrun)
    ca = prun
    d = 1
    while d < SUBLANES:
        cu = ca * jnp.where(sub >= d, pltpu.roll(cu, d, axis=0), 0.0) + cu
        if 2 * d < SUBLANES:
            ca = ca * jnp.where(sub >= d, pltpu.roll(ca, d, axis=0), 1.0)
        d *= 2
    h_enter = jnp.where(sub == 0, h_prev, pltpu.roll(cu, 1, axis=0))
    hlast_ref[...] = cu[SUBLANES - 1:]
    h2 = (hs_ref[...] + ps_ref[...] * h_enter[None]).reshape(tt, LRU_WIDTH)
    for c in range(LRU_WIDTH // LANES):
        hil_ref[c] = h2[:, c * LANES:(c + 1) * LANES]
    glr = _dot(ub_ref[...], wgate_ref[...])
    for c in range(LRU_WIDTH // LANES):
        for s in range(SUBLANES):
            rows = slice(s * seg, (s + 1) * seg)
            lanes = slice(c * LANES, (c + 1) * LANES)
            y_ref[rows, GLA_WIDTH + c * LANES:GLA_WIDTH + (c + 1) * LANES] = (
                _time_ordered_block(hil_ref, c, s) * _gelu_tanh(yg_ref[rows, lanes])).astype(BF16)
    out_ref[...] = h_ref[...] + _dot(y_ref[:, GLA_WIDTH:], wout_ref[GLA_WIDTH:, :])

    gate = _dot(glr.astype(BF16), w2_ref[...]) + b2_ref[...]
    log_alpha = -_softplus(-gate) * (1.0 / GLA_GATE_NORMALIZER)
    lail_ref = hil_ref.at[0:QK_COLS // LANES]
    bil_ref = xil_ref
    _interleave_rows(log_alpha, lail_ref)
    la3 = _slabs_value(lail_ref).reshape(seg, SUBLANES, QK_COLS)
    nq = QK_COLS // LANES
    for c0 in range(0, seg, GLA_CHUNK):
        run = la3[c0]
        cum = [run]
        for j in range(c0 + 1, c0 + GLA_CHUNK):
            run = run + la3[j]
            cum.append(run)
        for j in range(GLA_CHUNK):
            for c in range(nq):
                lanes = slice(c * LANES, (c + 1) * LANES)
                rows = slice((c0 + j) * SUBLANES, (c0 + j + 1) * SUBLANES)
                bil_ref[c, rows, :] = cum[j][:, lanes]
                bil_ref[nq + c, rows, :] = (run - cum[j])[:, lanes]
    for s in range(SUBLANES):
        rows = slice(s * seg, (s + 1) * seg)
        b = jnp.concatenate([_time_ordered_block(bil_ref, c, s) for c in range(nq)], axis=1)
        bd = jnp.concatenate([_time_ordered_block(bil_ref, nq + c, s) for c in range(nq)], axis=1)
        q = qkvg_ref[rows, Q0:Q0 + QK_COLS]
        k = qkvg_ref[rows, K0:K0 + QK_COLS]
        bnat_ref[rows, :] = b
        qi_ref[rows, :] = ((q * (GLA_DK ** -0.5)) * jnp.exp(b)).astype(BF16)
        ki_ref[rows, :] = (k * jnp.exp(-b)).astype(BF16)
        kd_ref[rows, :] = (k * jnp.exp(bd)).astype(BF16)

    lane_head = lax.broadcasted_iota(jnp.int32, (GLA_CHUNK, QK_COLS), 1) // GLA_DK
    head_masks = [lane_head == hd for hd in range(GLA_HEADS)]
    c_row = lax.broadcasted_iota(jnp.int32, (GLA_HEADS * GLA_CHUNK, GLA_CHUNK), 0) % GLA_CHUNK
    c_col = lax.broadcasted_iota(jnp.int32, (GLA_HEADS * GLA_CHUNK, GLA_CHUNK), 1)
    causal = c_row >= c_col

    def stack_heads(x):
        zero = jnp.zeros_like(x)
        return jnp.concatenate([jnp.where(m, x, zero) for m in head_masks], axis=0)

    gnorm = gnorm_ref[...]
    n_chunks = tt // GLA_CHUNK
    chunk_rows = lambda c: slice(c * GLA_CHUNK, (c + 1) * GLA_CHUNK)
    head_cols = lambda hd: slice(hd * GLA_DV, (hd + 1) * GLA_DV)

    for c in range(n_chunks):
        rows = chunk_rows(c)
        qs = stack_heads(qi_ref[rows, :])
        ks = stack_heads(kd_ref[rows, :])
        v = qkvg_ref[rows, V0:V0 + GLA_WIDTH].astype(BF16)
        a_ref[c] = jnp.where(causal, _dot_nt(qs, ki_ref[rows, :]), 0.0).astype(BF16)
        v_stack = jnp.concatenate([v[:, head_cols(hd)] for hd in range(GLA_HEADS)], axis=0)
        u_ref[c] = _dot_tn(v_stack, ks)

    st = st_ref[...]
    for c in range(n_chunks):
        sts_ref[c] = st.T.astype(BF16)
        decay = jnp.exp(bnat_ref[(c + 1) * GLA_CHUNK - 1:(c + 1) * GLA_CHUNK, :])
        st = decay * st + u_ref[c]
    st_ref[...] = st

    out_group = n_chunks // 2
    for c in range(n_chunks):
        rows = chunk_rows(c)
        qs = stack_heads(qi_ref[rows, :])
        v = qkvg_ref[rows, V0:V0 + GLA_WIDTH].astype(BF16)
        o_inter = _dot(qs, sts_ref[c])
        a_all = a_ref[c]
        for hd in range(GLA_HEADS):
            hr = slice(hd * GLA_CHUNK, (hd + 1) * GLA_CHUNK)
            vs = head_cols(hd)
            o = _dot(a_all[hr], v[:, vs]) + o_inter[hr]
            gh = qkvg_ref[rows, G0 + hd * GLA_DV:G0 + (hd + 1) * GLA_DV]
            y_ref[rows, vs] = (_rmsnorm(o, gnorm) * (gh * _sigmoid(gh))).astype(BF16)
        if (c + 1) % out_group == 0:
            orows = slice((c + 1 - out_group) * GLA_CHUNK, (c + 1) * GLA_CHUNK)
            out_ref[orows, :] = out_ref[orows, :] + _dot(y_ref[orows, :GLA_WIDTH], wout_ref[:GLA_WIDTH, :])


def _ffn_kernel(h_ref, ln_ref, win_ref, cw_ref, cb_ref, wdown_ref, lnf_ref, out_ref,
                hil_ref, oil_ref, ub_ref, tail_ref, act_ref, *, tt, final_norm):
    i = pl.program_id(0)
    seg = tt // SUBLANES
    hist = FFN_CONV - 1

    @pl.when(i == 0)
    def _init():
        tail_ref[...] = jnp.zeros_like(tail_ref)

    _interleave_rows(h_ref, hil_ref)
    n_slabs = D_MODEL // LANES
    for r0 in range(0, tt, NORM_ROWS):
        rows = slice(r0, r0 + NORM_ROWS)
        x = jnp.concatenate([hil_ref[c, rows, :] for c in range(n_slabs)], axis=1)
        ub_ref[rows, :] = _rmsnorm(x, ln_ref[...]).astype(BF16)

    def conv(c0):
        cols = slice(c0, c0 + FFN_COL_BLOCK)
        z = _dot(ub_ref[...], win_ref[:, cols]).reshape(seg, SUBLANES, FFN_COL_BLOCK)
        tail = z[seg - hist:]
        before = _segment_history(tail, tail_ref[:, :, cols])
        tail_ref[:, :, cols] = tail
        return _causal_conv_interleaved(z, before, cw_ref, cb_ref, cols).reshape(tt, FFN_COL_BLOCK)

    for blk in range(FFN_HIDDEN // FFN_COL_BLOCK):
        c0 = blk * FFN_COL_BLOCK
        za = conv(c0)
        zg = conv(FFN_HIDDEN + c0)
        act_ref[:, c0:c0 + FFN_COL_BLOCK] = (_gelu_tanh(za) * zg).astype(BF16)

    down = _dot(act_ref[...], wdown_ref[...])
    if final_norm:
        out = _rmsnorm(_slabs_value(hil_ref) + down, lnf_ref[...])
        for c in range(n_slabs):
            oil_ref[c] = out[:, c * LANES:(c + 1) * LANES]
    else:
        for c in range(n_slabs):
            oil_ref[c] = hil_ref[c] + down[:, c * LANES:(c + 1) * LANES]
    for c in range(n_slabs):
        for s in range(SUBLANES):
            out_ref[s * seg:(s + 1) * seg, c * LANES:(c + 1) * LANES] = _time_ordered_block(oil_ref, c, s)


def _const_spec(shape):
    zeros = (0,) * len(shape)
    return pl.BlockSpec(shape, lambda i: zeros, pipeline_mode=pl.Buffered(1))


def _layer_spec(arr, layer):
    zeros = (0,) * (arr.ndim - 1)
    return pl.BlockSpec((None,) + arr.shape[1:], lambda i: (layer,) + zeros, pipeline_mode=pl.Buffered(1))


def _layer_rows_spec(arr, layer, n_steps):
    rows = arr.shape[1] // n_steps
    assert rows * n_steps == arr.shape[1] and rows % 16 == 0
    return pl.BlockSpec((None, rows, arr.shape[2]), lambda i: (layer, i, 0))


def _rows_out(n_rows, n_cols, n_steps):
    rows = n_rows // n_steps
    return (jax.ShapeDtypeStruct((n_rows, n_cols), BF16), pl.BlockSpec((rows, n_cols), lambda i: (i, 0)))


def _time_tile(t):
    assert t % TIME_TILE == 0 and (TIME_TILE // SUBLANES) % GLA_CHUNK == 0
    return TIME_TILE


def _mixer_call(h, layer, wout, stacked, raw):
    t = h.shape[0]
    tt = _time_tile(t)
    seg = tt // SUBLANES
    n_steps = t // tt
    depth = raw[0].shape[0]
    cast_next = layer + 1 < depth
    ffn_w_in, ffn_w_down, w_out = raw
    row_spec = pl.BlockSpec((tt, D_MODEL), lambda i: (i, 0))
    in_specs = ([row_spec] + [_layer_spec(a, layer) for a in stacked] + [_const_spec(wout.shape)]
                + [_layer_rows_spec(ffn_w_in, layer, n_steps), _layer_rows_spec(ffn_w_down, layer, n_steps)])
    args = [h, *stacked, wout, ffn_w_in, ffn_w_down]
    outs = [(jax.ShapeDtypeStruct((t, D_MODEL), F32), row_spec),
            _rows_out(D_MODEL, 2 * FFN_HIDDEN, n_steps), _rows_out(FFN_HIDDEN, D_MODEL, n_steps)]
    if cast_next:
        in_specs.append(_layer_rows_spec(w_out, layer + 1, n_steps))
        args.append(w_out)
        outs.append(_rows_out(D_MODEL, D_MODEL, n_steps))
    res = pl.pallas_call(
        functools.partial(_mixer_kernel, tt=tt, cast_next=cast_next),
        grid=(n_steps,),
        in_specs=in_specs,
        out_specs=[o[1] for o in outs],
        out_shape=[o[0] for o in outs],
        scratch_shapes=[
            pltpu.VMEM((D_MODEL, 2 * LRU_WIDTH), BF16),
            pltpu.VMEM((D_MODEL, GATE_PAD), BF16),
            pltpu.VMEM((tt, D_MODEL), BF16),
            pltpu.VMEM((tt, X0), F32),
            pltpu.VMEM((tt, LRU_WIDTH), F32),
            pltpu.VMEM((LRU_WIDTH // LANES, tt, LANES), F32),
            pltpu.VMEM((tt, QK_COLS), F32),
            pltpu.VMEM((tt, QK_COLS), BF16),
            pltpu.VMEM((tt, QK_COLS), BF16),
            pltpu.VMEM((tt, QK_COLS), BF16),
            pltpu.VMEM((tt, D_MODEL), BF16),
            pltpu.VMEM((GLA_DV, QK_COLS), F32),
            pltpu.VMEM((tt // GLA_CHUNK, GLA_HEADS * GLA_CHUNK, GLA_CHUNK), BF16),
            pltpu.VMEM((tt // GLA_CHUNK, GLA_DV, QK_COLS), F32),
            pltpu.VMEM((tt // GLA_CHUNK, QK_COLS, GLA_DV), BF16),
            pltpu.VMEM((LRU_CONV - 1, SUBLANES, LRU_WIDTH), F32),
            pltpu.VMEM((seg, SUBLANES, LRU_WIDTH), F32),
            pltpu.VMEM((seg, SUBLANES, LRU_WIDTH), F32),
            pltpu.VMEM((LRU_WIDTH // LANES, tt, LANES), F32),
            pltpu.VMEM((1, LRU_WIDTH), F32),
        ],
        compiler_params=pltpu.CompilerParams(
            dimension_semantics=("arbitrary",), vmem_limit_bytes=MIXER_VMEM_LIMIT_BYTES),
        name="mixer",
    )(*args)
    return res[0], tuple(res[1:3]), (res[3] if cast_next else None)


def _ffn_call(h, layer, ffn_w, stacked, lnf, final_norm):
    t = h.shape[0]
    tt = _time_tile(t)
    win, wdown = ffn_w
    ln, cw, cb = stacked
    row_spec = pl.BlockSpec((tt, D_MODEL), lambda i: (i, 0))
    lspec = lambda a: _layer_spec(a, layer)
    return pl.pallas_call(
        functools.partial(_ffn_kernel, tt=tt, final_norm=final_norm),
        grid=(t // tt,),
        in_specs=[row_spec, lspec(ln), _const_spec(win.shape), lspec(cw), lspec(cb), _const_spec(wdown.shape),
                  _const_spec(lnf.shape)],
        out_specs=row_spec,
        out_shape=jax.ShapeDtypeStruct((t, D_MODEL), F32),
        scratch_shapes=[
            pltpu.VMEM((D_MODEL // LANES, tt, LANES), F32),
            pltpu.VMEM((D_MODEL // LANES, tt, LANES), F32),
            pltpu.VMEM((tt, D_MODEL), BF16),
            pltpu.VMEM((FFN_CONV - 1, SUBLANES, 2 * FFN_HIDDEN), F32),
            pltpu.VMEM((tt, FFN_HIDDEN), BF16),
        ],
        compiler_params=pltpu.CompilerParams(
            dimension_semantics=("arbitrary",), vmem_limit_bytes=VMEM_LIMIT_BYTES),
        name="ffn",
    )(h, ln, win, cw, cb, wdown, lnf)


def _block_diag_tiles(w):
    depth = w.shape[0]
    per = MXU_DIM // LRU_BLOCK
    w5 = w.reshape(depth, LRU_BLOCKS // per, per, LRU_BLOCK, LRU_BLOCK)
    tiles = jnp.einsum("lsjab,jk->lsjakb", w5, jnp.eye(per, dtype=w.dtype))
    return tiles.reshape(depth, LRU_BLOCKS // per, MXU_DIM, MXU_DIM).astype(BF16)


def kernel(x, ln_mix, w_in, gla_gate_w2, gla_gate_b, gla_norm, lru_conv_w, lru_conv_b,
           lru_wa, lru_ba, lru_wx, lru_bx, lru_lambda, w_out, ln_ffn, ffn_w_in,
           ffn_conv_w, ffn_conv_b, ffn_w_down, ln_final):
    bsz, t, d = x.shape
    assert bsz == 1 and d == D_MODEL and w_in.shape[2] == D_IN
    depth = w_in.shape[0]
    h = x.reshape(t, d)
    rows = lambda p: p.reshape(depth, 1, -1)
    w2 = jnp.pad(gla_gate_w2, ((0, 0), (0, GATE_PAD - GLA_GATE_RANK), (0, 0))).astype(BF16)
    mixer_stacked = (rows(ln_mix), w_in.astype(BF16), w2, rows(gla_gate_b), rows(gla_norm), lru_conv_w,
                     rows(lru_conv_b), _block_diag_tiles(lru_wa), rows(lru_ba), _block_diag_tiles(lru_wx),
                     rows(lru_bx), rows(lru_lambda))
    ffn_small = (rows(ln_ffn), ffn_conv_w, rows(ffn_conv_b))
    raw = (ffn_w_in, ffn_w_down, w_out)
    wout = w_out[0].astype(BF16)
    for l in range(depth):
        h, ffn_w, wout = _mixer_call(h, l, wout, mixer_stacked, raw)
        h = _ffn_call(h, l, ffn_w, ffn_small, ln_final.reshape(1, -1), final_norm=(l == depth - 1))
    return h.reshape(bsz, t, d)
```
